```python
import functools
import jax, jax.numpy as jnp
from jax import lax
import numpy as np

D_MODEL = 1024
BATCH = 4
SEQ = 8192
DEPTH = 2
DEC_BATCH = 128
DEC_SEQ = 4
PAST_LEN = 16384
PAGE_SIZE = 128

GROUP_WIDTH = D_MODEL // 4
HEAD_DIM = 64
N_HEADS_GRP = GROUP_WIDTH // HEAD_DIM
CHUNK = 128
CONV_WIDTH = 4
LRU_C = 8.0
MLA_HEADS = N_HEADS_GRP
Q_LORA = GROUP_WIDTH
KV_LORA = GROUP_WIDTH
QK_NOPE = HEAD_DIM
QK_ROPE = HEAD_DIM // 2
V_HEAD = HEAD_DIM
MLA_SCALE = (QK_NOPE + QK_ROPE) ** -0.5
ROPE_THETA = 10000.0
Q_BLOCK = 128
POOL_WINDOWS = (2, 4, 8, 16)
POOL_BUF = 15
N_POOL_GROUPS = 4
POOL_GC = GROUP_WIDTH // N_POOL_GROUPS
P_IN = 4 * GROUP_WIDTH + Q_LORA + KV_LORA + QK_ROPE + GROUP_WIDTH
D_FF = ((8 * D_MODEL // 3 + 127) // 128) * 128
N_EXPERTS = 8
TOP_K = 2
N_DENSE = (DEPTH + 1) // 2
N_MOE = DEPTH // 2
ALPHA = (2 * DEPTH) ** 0.25
BETA = (8 * DEPTH) ** -0.25
LN_EPS = 1e-5
RMS_EPS = 1e-6
F32 = jnp.float32

kernel_name = 'hymba_style_hybrid_decoder_step'


def proj_splits():
    sizes = [GROUP_WIDTH, GROUP_WIDTH, GROUP_WIDTH, GROUP_WIDTH, Q_LORA, KV_LORA, QK_ROPE, GROUP_WIDTH]
    return [int(s) for s in np.cumsum(sizes)[:-1]]


def layer_norm(x, g=None, b=None):
    xf = x.astype(F32)
    mu = jnp.mean(xf, -1, keepdims=True)
    var = jnp.mean(jnp.square(xf - mu), -1, keepdims=True)
    y = (xf - mu) * lax.rsqrt(var + LN_EPS)
    if g is not None:
        y = y * g.astype(F32) + b.astype(F32)
    return y.astype(x.dtype)


def rms_norm(x, g):
    xf = x.astype(F32)
    y = xf * lax.rsqrt(jnp.mean(xf * xf, -1, keepdims=True) + RMS_EPS)
    return (y * g.astype(F32)).astype(x.dtype)


def rope(x, pos):
    half = x.shape[-1] // 2
    freq = ROPE_THETA ** (-jnp.arange(half, dtype=F32) / half)
    ang = pos.astype(F32)[:, None] * freq[None, :]
    shape = (1, ang.shape[0]) + (1,) * (x.ndim - 3) + (half,)
    cos = jnp.cos(ang).reshape(shape)
    sin = jnp.sin(ang).reshape(shape)
    xf = x.astype(F32)
    x1, x2 = xf[..., :half], xf[..., half:]
    return jnp.concatenate([x1 * cos - x2 * sin, x2 * cos + x1 * sin], -1).astype(x.dtype)


def chunk_mlp(u, v, ln_g, ln_b, w_s, b_s):
    B, T, _ = u.shape
    L = min(T, CHUNK)
    u = jax.nn.gelu(u)
    v = layer_norm(jax.nn.gelu(v), ln_g, ln_b)
    mask = jnp.tril(jnp.ones((L, L), dtype=bool))
    w = jnp.where(mask[None], w_s[:, :L, :L], 0).astype(v.dtype)
    vc = v.reshape(B, T // L, L, N_HEADS_GRP, HEAD_DIM)
    mixed = jnp.einsum('hts,bnshd->bnthd', w, vc) + b_s[:, :L].T[None, None, :, :, None]
    return u * mixed.reshape(B, T, GROUP_WIDTH), v


def rglru_block(xb, gb, conv_buf, h0, conv_w, conv_b, wa, ba, wx, bx, lam):
    B, T, C = xb.shape
    z = jnp.concatenate([conv_buf.astype(xb.dtype), xb], 1)
    xc = conv_b + z[:, 0:T] * conv_w[0]
    for k in range(1, CONV_WIDTH):
        xc = xc + z[:, k:k + T] * conv_w[k]
    new_buf = z[:, T:]
    xh = xc.reshape(B, T, N_HEADS_GRP, HEAD_DIM)
    r = jax.nn.sigmoid((jnp.einsum('btgc,gcd->btgd', xh, wa).reshape(B, T, C) + ba).astype(F32))
    i = jax.nn.sigmoid((jnp.einsum('btgc,gcd->btgd', xh, wx).reshape(B, T, C) + bx).astype(F32))
    log_a = -LRU_C * r * jax.nn.softplus(-lam.astype(F32))
    a = jnp.exp(log_a)
    bt = jnp.sqrt(-jnp.expm1(2.0 * log_a)) * i * xc.astype(F32)
    bt = bt.at[:, 0].add(a[:, 0] * h0.astype(F32))

    def combine(left, right):
        return left[0] * right[0], right[0] * left[1] + right[1]

    _, h = lax.associative_scan(combine, (a, bt), axis=1)
    out = h.astype(xb.dtype) * jax.nn.gelu(gb)
    return out, new_buf, h[:, -1].astype(xb.dtype)


def multi_pool(p, pool_buf, start, pool_w, pool_scale):
    B, T, C = p.shape
    z = jnp.concatenate([pool_buf.astype(p.dtype), p], 1)
    cs = jnp.cumsum(z.astype(F32), axis=1)
    cs = jnp.concatenate([jnp.zeros((B, 1, C), F32), cs], 1)
    pos = start + jnp.arange(T)
    means = []
    for g, w in enumerate(POOL_WINDOWS):
        sl = slice(g * POOL_GC, (g + 1) * POOL_GC)
        end = cs[:, POOL_BUF + 1:, sl]
        beg = cs[:, POOL_BUF + 1 - w:POOL_BUF + 1 - w + T, sl]
        cnt = jnp.minimum(pos + 1, w).astype(F32)[None, :, None]
        means.append((end - beg) / cnt)
    mean = jnp.concatenate(means, -1)
    d = (mean - p.astype(F32)).astype(p.dtype).reshape(B, T, N_POOL_GROUPS, POOL_GC)
    out = jnp.einsum('btgc,gcd->btgd', d, pool_w).reshape(B, T, C) * pool_scale
    return out, z[:, T:]


def mla_project(cq, ckv_raw, kr_raw, pos, q_norm, w_uq, kv_norm, w_uk):
    q = jnp.einsum('btc,chd->bthd', rms_norm(cq, q_norm), w_uq)
    q_nope, q_rope = q[..., :QK_NOPE], q[..., QK_NOPE:]
    q_lat = jnp.einsum('bthd,chd->bthc', q_nope, w_uk)
    q_rope = rope(q_rope, pos)
    ckv = rms_norm(ckv_raw, kv_norm)
    kr = rope(kr_raw[:, :, None, :], pos)[:, :, 0]
    return q_lat, q_rope, ckv, kr


def mla_attend_prompt(q_lat, q_rope, ckv, kr):
    B, T, H, _ = q_lat.shape
    nb = T // Q_BLOCK
    ql = q_lat.reshape(B, nb, Q_BLOCK, H, KV_LORA).transpose(1, 0, 2, 3, 4)
    qr = q_rope.reshape(B, nb, Q_BLOCK, H, QK_ROPE).transpose(1, 0, 2, 3, 4)
    kpos = jnp.arange(T)

    def block(args):
        qlb, qrb, bi = args
        s = (jnp.einsum('bqhc,bkc->bhqk', qlb, ckv) + jnp.einsum('bqhr,bkr->bhqk', qrb, kr)).astype(F32) * MLA_SCALE
        qpos = bi * Q_BLOCK + jnp.arange(Q_BLOCK)
        s = jnp.where(kpos[None, :] <= qpos[:, None], s, -jnp.inf)
        pr = jax.nn.softmax(s, axis=-1).astype(ckv.dtype)
        return jnp.einsum('bhqk,bkc->bqhc', pr, ckv)

    o = lax.map(block, (ql, qr, jnp.arange(nb)))
    return o.transpose(1, 0, 2, 3, 4).reshape(B, T, H, KV_LORA)


def mla_attend_sample(q_lat, q_rope, ckv, kr, past_ckv, past_kr):
    T = q_lat.shape[1]
    P = past_ckv.shape[1]
    s_past = (jnp.einsum('bqhc,bkc->bhqk', q_lat, past_ckv) + jnp.einsum('bqhr,bkr->bhqk', q_rope, past_kr)).astype(F32) * MLA_SCALE
    s_new = (jnp.einsum('bqhc,bkc->bhqk', q_lat, ckv) + jnp.einsum('bqhr,bkr->bhqk', q_rope, kr)).astype(F32) * MLA_SCALE
    s_new = jnp.where(jnp.tril(jnp.ones((T, T), dtype=bool)), s_new, -jnp.inf)
    pr = jax.nn.softmax(jnp.concatenate([s_past, s_new], -1), axis=-1).astype(ckv.dtype)
    return jnp.einsum('bhqk,bkc->bqhc', pr[..., :P], past_ckv) + jnp.einsum('bhqk,bkc->bqhc', pr[..., P:], ckv)


def swiglu(h, w1, w3, w2):
    return (jax.nn.silu(h @ w1) * (h @ w3)) @ w2


def moe_ffn(h, router_w, router_b, w1, w3, w2):
    B, T, D = h.shape
    hf = h.reshape(B * T, D)
    logits = (hf @ router_w + router_b).astype(F32)
    top_v, top_i = lax.top_k(logits, TOP_K)
    gates = jax.nn.softmax(top_v, axis=-1)
    dense_g = jnp.sum(jax.nn.one_hot(top_i, N_EXPERTS, dtype=F32) * gates[..., None], axis=1).astype(h.dtype)
    y = jnp.zeros_like(hf)
    for e in range(N_EXPERTS):
        y = y + dense_g[:, e:e + 1] * swiglu(hf, w1[e], w3[e], w2[e])
    return y.reshape(B, T, D)


def layer_forward(x, c, start, conv_buf, h0, pool_buf, attend, ffn, p):
    B, T, _ = x.shape
    pos = start + jnp.arange(T)
    mod = (jax.nn.silu(c) @ p['ada_w'] + p['ada_b'])[:, None, :]
    sh1, sc1, g1, sh2, sc2, g2 = jnp.split(mod, 6, axis=-1)
    h = layer_norm(x) * (1 + sc1) + sh1
    u, v, xb, gb, cq, ckv_raw, kr_raw, pd = jnp.split(h @ p['w_in'], proj_splits(), axis=-1)
    a_out, v_rows = chunk_mlp(u, v, p['sgu_ln_g'], p['sgu_ln_b'], p['sgu_w'], p['sgu_b'])
    b_out, new_conv, new_h = rglru_block(xb, gb, conv_buf, h0, p['lru_conv_w'], p['lru_conv_b'],
                                         p['lru_wa'], p['lru_ba'], p['lru_wx'], p['lru_bx'], p['lru_lambda'])
    q_lat, q_rope, ckv, kr = mla_project(cq, ckv_raw, kr_raw, pos, p['mla_q_norm'], p['mla_w_uq'],
                                         p['mla_kv_norm'], p['mla_w_uk'])
    o_lat = attend(q_lat, q_rope, ckv, kr)
    c_out = jnp.einsum('bthc,chd->bthd', o_lat, p['mla_w_uv']).reshape(B, T, GROUP_WIDTH)
    d_out, new_pool = multi_pool(pd, pool_buf, start, p['pool_w'], p['pool_scale'])
    mix = jnp.concatenate([a_out, b_out, c_out, d_out], -1) @ p['w_out']
    x = layer_norm(ALPHA * x + g1 * mix, p['ln1_g'], p['ln1_b'])
    h2 = layer_norm(x) * (1 + sc2) + sh2
    x = layer_norm(ALPHA * x + g2 * ffn(h2), p['ln2_g'], p['ln2_b'])
    return x, (ckv, kr, new_conv, new_h, new_pool, v_rows)


def setup_inputs(seed: int = 0) -> dict:
    key = jax.random.key(seed)
    keys = list(jax.random.split(key, 64))

    def nrm(shape, scale=1.0):
        return jax.random.normal(keys.pop(), shape, F32) * scale

    def gain(shape):
        return 1.0 + nrm(shape, 0.02)

    n_pages = PAST_LEN // PAGE_SIZE
    n_pool = (DEC_BATCH * n_pages * 5) // 4
    page_table = jax.random.permutation(keys.pop(), n_pool)[:DEC_BATCH * n_pages].reshape(DEC_BATCH, n_pages).astype(jnp.int32)
    a_c = jax.random.uniform(keys.pop(), (DEPTH, GROUP_WIDTH), F32, 0.9, 0.999)
    a_base = a_c ** (1.0 / LRU_C)
    lru_lambda = jnp.log(a_base) - jnp.log1p(-a_base)
    return {
        'x_prompt': nrm((BATCH, SEQ, D_MODEL)),
        'x_sample': nrm((DEC_BATCH, DEC_SEQ, D_MODEL)),
        'c_prompt': nrm((BATCH, D_MODEL)),
        'c_sample': nrm((DEC_BATCH, D_MODEL)),
        'cache_ckv': nrm((DEPTH, n_pool, PAGE_SIZE, KV_LORA)),
        'cache_kr': nrm((DEPTH, n_pool, PAGE_SIZE, QK_ROPE)),
        'page_table': page_table,
        'state_lru_conv': nrm((DEPTH, DEC_BATCH, CONV_WIDTH - 1, GROUP_WIDTH)),
        'state_lru_h': nrm((DEPTH, DEC_BATCH, GROUP_WIDTH), 0.5),
        'state_pool': nrm((DEPTH, DEC_BATCH, POOL_BUF, GROUP_WIDTH)),
        'ada_w': nrm((DEPTH, D_MODEL, 6 * D_MODEL), 0.5 * D_MODEL ** -0.5),
        'ada_b': nrm((DEPTH, 6 * D_MODEL), 0.02),
        'w_in': nrm((DEPTH, D_MODEL, P_IN), D_MODEL ** -0.5),
        'w_out': nrm((DEPTH, D_MODEL, D_MODEL), BETA * D_MODEL ** -0.5),
        'ln1_g': gain((DEPTH, D_MODEL)),
        'ln1_b': nrm((DEPTH, D_MODEL), 0.02),
        'ln2_g': gain((DEPTH, D_MODEL)),
        'ln2_b': nrm((DEPTH, D_MODEL), 0.02),
        'sgu_ln_g': gain((DEPTH, GROUP_WIDTH)),
        'sgu_ln_b': nrm((DEPTH, GROUP_WIDTH), 0.02),
        'sgu_w': nrm((DEPTH, N_HEADS_GRP, CHUNK, CHUNK), 0.5 * CHUNK ** -0.5),
        'sgu_b': gain((DEPTH, N_HEADS_GRP, CHUNK)),
        'lru_conv_w': nrm((DEPTH, CONV_WIDTH, GROUP_WIDTH), CONV_WIDTH ** -0.5),
        'lru_conv_b': nrm((DEPTH, GROUP_WIDTH), 0.02),
        'lru_wa': nrm((DEPTH, N_HEADS_GRP, HEAD_DIM, HEAD_DIM), HEAD_DIM ** -0.5),
        'lru_ba': nrm((DEPTH, GROUP_WIDTH), 0.02),
        'lru_wx': nrm((DEPTH, N_HEADS_GRP, HEAD_DIM, HEAD_DIM), HEAD_DIM ** -0.5),
        'lru_bx': nrm((DEPTH, GROUP_WIDTH), 0.02),
        'lru_lambda': lru_lambda,
        'mla_q_norm': gain((DEPTH, Q_LORA)),
        'mla_w_uq': nrm((DEPTH, Q_LORA, MLA_HEADS, QK_NOPE + QK_ROPE), Q_LORA ** -0.5),
        'mla_kv_norm': gain((DEPTH, KV_LORA)),
        'mla_w_uk': nrm((DEPTH, KV_LORA, MLA_HEADS, QK_NOPE), KV_LORA ** -0.5),
        'mla_w_uv': nrm((DEPTH, KV_LORA, MLA_HEADS, V_HEAD), KV_LORA ** -0.5),
        'pool_w': nrm((DEPTH, N_POOL_GROUPS, POOL_GC, POOL_GC), POOL_GC ** -0.5),
        'pool_scale': gain((DEPTH, GROUP_WIDTH)),
        'ffn_w1': nrm((N_DENSE, D_MODEL, D_FF), D_MODEL ** -0.5),
        'ffn_w3': nrm((N_DENSE, D_MODEL, D_FF), D_MODEL ** -0.5),
        'ffn_w2': nrm((N_DENSE, D_FF, D_MODEL), BETA * D_FF ** -0.5),
        'moe_router': nrm((N_MOE, D_MODEL, N_EXPERTS), D_MODEL ** -0.5),
        'moe_router_b': nrm((N_MOE, N_EXPERTS), 0.01),
        'moe_w1': nrm((N_MOE, N_EXPERTS, D_MODEL, D_FF), D_MODEL ** -0.5),
        'moe_w3': nrm((N_MOE, N_EXPERTS, D_MODEL, D_FF), D_MODEL ** -0.5),
        'moe_w2': nrm((N_MOE, N_EXPERTS, D_FF, D_MODEL), BETA * D_FF ** -0.5),
    }


def reference(x_prompt, x_sample, c_prompt, c_sample, cache_ckv, cache_kr, page_table,
              state_lru_conv, state_lru_h, state_pool, ada_w, ada_b, w_in, w_out,
              ln1_g, ln1_b, ln2_g, ln2_b, sgu_ln_g, sgu_ln_b, sgu_w, sgu_b,
              lru_conv_w, lru_conv_b, lru_wa, lru_ba, lru_wx, lru_bx, lru_lambda,
              mla_q_norm, mla_w_uq, mla_kv_norm, mla_w_uk, mla_w_uv, pool_w, pool_scale,
              ffn_w1, ffn_w3, ffn_w2, moe_router, moe_router_b, moe_w1, moe_w3, moe_w2):
    bp = x_prompt.shape[0]
    bd, n_pages = page_table.shape
    xp, xs = x_prompt, x_sample
    sp = [[] for _ in range(6)]
    ss = [[] for _ in range(6)]
    for l in range(DEPTH):
        p = {
            'ada_w': ada_w[l], 'ada_b': ada_b[l], 'w_in': w_in[l], 'w_out': w_out[l],
            'ln1_g': ln1_g[l], 'ln1_b': ln1_b[l], 'ln2_g': ln2_g[l], 'ln2_b': ln2_b[l],
            'sgu_ln_g': sgu_ln_g[l], 'sgu_ln_b': sgu_ln_b[l], 'sgu_w': sgu_w[l], 'sgu_b': sgu_b[l],
            'lru_conv_w': lru_conv_w[l], 'lru_conv_b': lru_conv_b[l], 'lru_wa': lru_wa[l],
            'lru_ba': lru_ba[l], 'lru_wx': lru_wx[l], 'lru_bx': lru_bx[l], 'lru_lambda': lru_lambda[l],
            'mla_q_norm': mla_q_norm[l], 'mla_w_uq': mla_w_uq[l], 'mla_kv_norm': mla_kv_norm[l],
            'mla_w_uk': mla_w_uk[l], 'mla_w_uv': mla_w_uv[l],
            'pool_w': pool_w[l], 'pool_scale': pool_scale[l],
        }
        j = l // 2
        if l % 2 == 0:
            ffn = functools.partial(swiglu, w1=ffn_w1[j], w3=ffn_w3[j], w2=ffn_w2[j])
        else:
            ffn = functools.partial(moe_ffn, router_w=moe_router[j], router_b=moe_router_b[j],
                                    w1=moe_w1[j], w3=moe_w3[j], w2=moe_w2[j])
        xp, st_p = layer_forward(
            xp, c_prompt, 0,
            jnp.zeros((bp, CONV_WIDTH - 1, GROUP_WIDTH), xp.dtype),
            jnp.zeros((bp, GROUP_WIDTH), xp.dtype),
            jnp.zeros((bp, POOL_BUF, GROUP_WIDTH), xp.dtype),
            mla_attend_prompt, ffn, p)
        past_ckv = cache_ckv[l, page_table].reshape(bd, n_pages * PAGE_SIZE, KV_LORA)
        past_kr = cache_kr[l, page_table].reshape(bd, n_pages * PAGE_SIZE, QK_ROPE)
        attend_s = functools.partial(mla_attend_sample, past_ckv=past_ckv, past_kr=past_kr)
        xs, st_s = layer_forward(xs, c_sample, PAST_LEN, state_lru_conv[l], state_lru_h[l],
                                 state_pool[l], attend_s, ffn, p)
        for k in range(6):
            sp[k].append(st_p[k])
            ss[k].append(st_s[k])
    new_ckv_prompt = jnp.stack(sp[0])
    new_kr_prompt = jnp.stack(sp[1])
    new_lru_conv_prompt = jnp.stack(sp[2])
    new_lru_h_prompt = jnp.stack(sp[3])
    new_pool_prompt = jnp.stack(sp[4])
    new_ckv_sample = jnp.stack(ss[0])
    new_kr_sample = jnp.stack(ss[1])
    new_lru_conv_sample = jnp.stack(ss[2])
    new_lru_h_sample = jnp.stack(ss[3])
    new_pool_sample = jnp.stack(ss[4])
    new_chunk_v_sample = jnp.stack(ss[5])
    return (xp, xs, new_ckv_prompt, new_kr_prompt, new_lru_conv_prompt, new_lru_h_prompt, new_pool_prompt,
            new_ckv_sample, new_kr_sample, new_lru_conv_sample, new_lru_h_sample, new_pool_sample,
            new_chunk_v_sample)
```

```python
import functools

import jax
import jax.numpy as jnp
import numpy as np
from jax import lax
from jax.experimental import pallas as pl
from jax.experimental.pallas import tpu as pltpu

F32 = jnp.float32
BF16 = jnp.bfloat16

D_MODEL = 1024
DEPTH = 2
PAGE_SIZE = 128
GROUP_WIDTH = D_MODEL // 4
HEAD_DIM = 64
N_HEADS = GROUP_WIDTH // HEAD_DIM
CHUNK = 128
CONV_WIDTH = 4
LRU_C = 8.0
QK_NOPE = HEAD_DIM
QK_ROPE = HEAD_DIM // 2
MLA_SCALE = (QK_NOPE + QK_ROPE) ** -0.5
ROPE_THETA = 10000.0
POOL_WINDOWS = (2, 4, 8, 16)
POOL_BUF = 15
POOL_GC = GROUP_WIDTH // 4
D_FF = ((8 * D_MODEL // 3 + 127) // 128) * 128
N_EXPERTS = 8
ALPHA = (2 * DEPTH) ** 0.25
LN_EPS = 1e-5
RMS_EPS = 1e-6

GW = GROUP_WIDTH
OFF_U, OFF_V, OFF_XB, OFF_GB, OFF_CQ, OFF_CKV, OFF_PD, OFF_KRA, OFF_KRB = (
    0, GW, 2 * GW, 3 * GW, 4 * GW, 5 * GW, 6 * GW, 7 * GW, 7 * GW + 128)
P_IN_PAD = 8 * GW
ROPE_LANES = N_HEADS * QK_ROPE

V_SGU_G, V_SGU_B, V_CONV_B, V_LAM, V_QNORM, V_KVNORM, V_PSCALE = 0, 1, 2, 3, 4, 5, 6
V_CONV_W = 8
VEC_ROWS = 16

VMEM_LIMIT_BYTES = 56 * 1024 * 1024
PROMPT_TILE = 256
ATTN_TILE = 256
DECODE_PAGES = 16
NEW_KEY_PAD = 16


def _dot(a, b):
    return jnp.dot(a, b, preferred_element_type=F32)


def _dot_nt(a, b):
    return lax.dot_general(a, b, (((1,), (1,)), ((), ())), preferred_element_type=F32)


def _ln(x):
    mu = jnp.mean(x, -1, keepdims=True)
    xc = x - mu
    var = jnp.mean(xc * xc, -1, keepdims=True)
    return xc * lax.rsqrt(var + LN_EPS)


def _rms(x):
    return x * lax.rsqrt(jnp.mean(x * x, -1, keepdims=True) + RMS_EPS)


def _gelu(x):
    return 0.5 * x * (1.0 + jnp.tanh(0.7978845608028654 * (x + 0.044715 * (x * x * x))))


def _silu(x):
    return x * jax.nn.sigmoid(x)


def _softplus(x):
    return jnp.maximum(x, 0.0) + jnp.log1p(jnp.exp(-jnp.abs(x)))


def _lane_group(width, group):
    assert group & (group - 1) == 0
    return lax.broadcasted_iota(jnp.int32, (1, width), 1) >> (group.bit_length() - 1)


def _mod(x, n):
    return x & (n - 1) if n & (n - 1) == 0 else lax.rem(x, n)


def _cparams(sem):
    return pltpu.CompilerParams(dimension_semantics=sem, vmem_limit_bytes=VMEM_LIMIT_BYTES)


def _full(shape):
    n = len(shape)
    return pl.BlockSpec(shape, lambda *_: (0,) * n)


def _ada_kernel(c_ref, w_ref, b_ref, o_ref):
    o_ref[...] = _dot(_silu(c_ref[...]).astype(BF16), w_ref[...]) + b_ref[...]


def _ada(c_all, w, b):
    m, d = c_all.shape
    n = w.shape[1]
    tn = 512
    return pl.pallas_call(
        _ada_kernel,
        grid=(n // tn,),
        in_specs=[pl.BlockSpec((m, d), lambda j: (0, 0)),
                  pl.BlockSpec((d, tn), lambda j: (0, j)),
                  pl.BlockSpec((1, tn), lambda j: (0, j))],
        out_specs=pl.BlockSpec((m, tn), lambda j: (0, j)),
        out_shape=jax.ShapeDtypeStruct((m, n), F32),
        compiler_params=_cparams(("arbitrary",)),
        name="ada",
    )(c_all, w, b)


def _lru_gates(xc, wgate, bgate, lam):
    g = _dot(xc.astype(BF16), wgate) + bgate
    r = jax.nn.sigmoid(g[:, :GW])
    i = jax.nn.sigmoid(g[:, GW:])
    log_a = (-LRU_C) * r * _softplus(-lam)
    a = jnp.exp(log_a)
    bt = jnp.sqrt(-jnp.tanh(log_a) * (a * a + 1.0)) * i * xc
    return a, bt


def _mla_queries(cq, qnorm, wq, wuk, cos, sin):
    q = _dot((_rms(cq) * qnorm).astype(BF16), wq)
    qlat = _dot(q[:, :GW].astype(BF16), wuk) * MLA_SCALE
    qr = (q[:, GW:GW + ROPE_LANES] * cos + q[:, GW + ROPE_LANES:] * sin) * MLA_SCALE
    return qlat, qr


def _pool_select(s2, s4, s8, s16):
    grp = _lane_group(GW, POOL_GC)
    return jnp.where(grp == 0, s2, jnp.where(grp == 1, s4, jnp.where(grp == 2, s8, s16)))


def _pool_windows():
    grp = _lane_group(GW, POOL_GC)
    w = jnp.where(grp == 0, POOL_WINDOWS[0],
                  jnp.where(grp == 1, POOL_WINDOWS[1],
                            jnp.where(grp == 2, POOL_WINDOWS[2], POOL_WINDOWS[3])))
    return w.astype(F32)


def _premix_prompt_kernel(x_ref, mod_ref, win_ref, sguw_ref, sgub_ref, vec_ref, wgate_ref, bgate_ref,
                          wq_ref, wuk_ref, wpool_ref, cos_ref, sin_ref,
                          abd_ref, qlat_ref, qrope_ref, ckvf_ref, ckvb_ref, krf_ref, krb_ref,
                          nconv_ref, nh_ref, npool_ref,
                          zc_ref, zp_ref, hc_ref, *, tm):
    t = pl.program_id(1)

    @pl.when(t == 0)
    def _():
        zc_ref[0:8, :] = jnp.zeros((8, GW), F32)
        zp_ref[0:16, :] = jnp.zeros((16, GW), F32)
        hc_ref[...] = jnp.zeros((1, GW), F32)

    def vec(i):
        return vec_ref[i:i + 1, :]

    x = x_ref[0]
    sh1 = mod_ref[0, :, 0:D_MODEL]
    sc1 = mod_ref[0, :, D_MODEL:2 * D_MODEL]
    h = _ln(x) * (1.0 + sc1) + sh1
    proj = _dot(h.astype(BF16), win_ref[...])

    ug = _gelu(proj[:, OFF_U:OFF_U + GW])
    vg = _ln(_gelu(proj[:, OFF_V:OFF_V + GW])) * vec(V_SGU_G) + vec(V_SGU_B)
    vgb = vg.astype(BF16)
    head = _lane_group(GW, HEAD_DIM)
    row = lax.broadcasted_iota(jnp.int32, (CHUNK, CHUNK), 0)
    col = lax.broadcasted_iota(jnp.int32, (CHUNK, CHUNK), 1)
    wmix = [jnp.where(col <= row, sguw_ref[hh], 0.0).astype(BF16) for hh in range(N_HEADS)]
    parts = []
    for c in range(tm // CHUNK):
        vc = vgb[c * CHUNK:(c + 1) * CHUNK]
        acc = _dot(wmix[0], vc)
        for hh in range(1, N_HEADS):
            acc = jnp.where(head == hh, _dot(wmix[hh], vc), acc)
        parts.append(acc + sgub_ref[...])
    mixed = jnp.concatenate(parts, 0) if len(parts) > 1 else parts[0]
    abd_ref[0, :, 0:GW] = (ug * mixed).astype(BF16)

    xb = proj[:, OFF_XB:OFF_XB + GW]
    zc_ref[8:8 + tm, :] = xb
    z = zc_ref[...]
    xc = (vec(V_CONV_B) + vec(V_CONV_W + 3) * z + vec(V_CONV_W + 2) * pltpu.roll(z, 1, 0)
          + vec(V_CONV_W + 1) * pltpu.roll(z, 2, 0) + vec(V_CONV_W) * pltpu.roll(z, 3, 0))[8:]
    nconv_ref[0] = zc_ref[pl.ds(8 + tm - (CONV_WIDTH - 1), CONV_WIDTH - 1), :]
    zc_ref[0:8, :] = xb[tm - 8:tm]
    a, bt = _lru_gates(xc, wgate_ref[...], bgate_ref[...], vec(V_LAM))
    rowi = lax.broadcasted_iota(jnp.int32, (tm, GW), 0)
    k = 1
    while k < tm:
        keep = rowi >= k
        bt = jnp.where(keep, a * pltpu.roll(bt, k, 0) + bt, bt)
        a = jnp.where(keep, a * pltpu.roll(a, k, 0), a)
        k *= 2
    hs = a * hc_ref[...] + bt
    hc_ref[...] = hs[tm - 1:tm]
    nh_ref[0] = hs[tm - 1:tm]
    abd_ref[0, :, GW:2 * GW] = (hs * _gelu(proj[:, OFF_GB:OFF_GB + GW])).astype(BF16)

    cos = cos_ref[...]
    sin = sin_ref[...]
    qlat, qr = _mla_queries(proj[:, OFF_CQ:OFF_CQ + GW], vec(V_QNORM), wq_ref[...], wuk_ref[...], cos, sin)
    rope_head = _lane_group(ROPE_LANES, QK_ROPE)
    for hh in range(N_HEADS):
        qlat_ref[0, hh] = qlat[:, hh * GW:(hh + 1) * GW].astype(BF16)
        qrope_ref[0, hh] = jnp.where(rope_head == hh, qr, 0.0).astype(BF16)
    ckv = _rms(proj[:, OFF_CKV:OFF_CKV + GW]) * vec(V_KVNORM)
    ckvf_ref[0] = ckv
    ckvb_ref[0] = ckv.astype(BF16)
    kr = proj[:, OFF_KRA:OFF_KRA + ROPE_LANES] * cos + proj[:, OFF_KRB:OFF_KRB + ROPE_LANES] * sin
    krf_ref[0] = kr[:, 0:QK_ROPE]
    krb_ref[0] = kr.astype(BF16)

    pd = proj[:, OFF_PD:OFF_PD + GW]
    zp_ref[16:16 + tm, :] = pd
    zz = zp_ref[...]
    s2 = zz + pltpu.roll(zz, 1, 0)
    s4 = s2 + pltpu.roll(s2, 2, 0)
    s8 = s4 + pltpu.roll(s4, 4, 0)
    s16 = s8 + pltpu.roll(s8, 8, 0)
    wsum = _pool_select(s2, s4, s8, s16)[16:]
    pos = (t * tm + lax.broadcasted_iota(jnp.int32, (tm, 1), 0) + 1).astype(F32)
    cnt = jnp.minimum(pos, _pool_windows())
    d = (wsum / cnt - pd).astype(BF16)
    abd_ref[0, :, 2 * GW:3 * GW] = (_dot(d, wpool_ref[...]) * vec(V_PSCALE)).astype(BF16)
    npool_ref[0] = zp_ref[pl.ds(16 + tm - POOL_BUF, POOL_BUF), :]
    zp_ref[0:16, :] = pd[tm - 16:tm]


def _premix_prompt(x, mod, lw, cos, sin):
    b, t, d = x.shape
    tm = PROMPT_TILE
    assert t % tm == 0 and tm % CHUNK == 0
    grid = (b, t // tm)
    tok = lambda w: pl.BlockSpec((1, tm, w), lambda i, j: (i, j, 0))
    per_b = lambda r: pl.BlockSpec((1, r, GW), lambda i, j: (i, 0, 0))
    in_specs = [
        tok(d),
        pl.BlockSpec((1, 1, 2 * d), lambda i, j: (i, 0, 0)),
        _full(lw["w_in"].shape), _full(lw["sgu_w"].shape), _full(lw["sgu_bias"].shape),
        _full(lw["vec"].shape), _full(lw["w_gate"].shape), _full(lw["b_gate"].shape),
        _full(lw["w_q"].shape), _full(lw["w_uk"].shape), _full(lw["w_pool"].shape),
        pl.BlockSpec((tm, ROPE_LANES), lambda i, j: (j, 0)),
        pl.BlockSpec((tm, ROPE_LANES), lambda i, j: (j, 0)),
    ]
    out_specs = [
        tok(3 * GW),
        pl.BlockSpec((1, N_HEADS, tm, GW), lambda i, j: (i, 0, j, 0)),
        pl.BlockSpec((1, N_HEADS, tm, ROPE_LANES), lambda i, j: (i, 0, j, 0)),
        tok(GW), tok(GW), tok(QK_ROPE), tok(ROPE_LANES),
        per_b(CONV_WIDTH - 1), per_b(1), per_b(POOL_BUF),
    ]
    out_shape = [
        jax.ShapeDtypeStruct((b, t, 3 * GW), BF16),
        jax.ShapeDtypeStruct((b, N_HEADS, t, GW), BF16),
        jax.ShapeDtypeStruct((b, N_HEADS, t, ROPE_LANES), BF16),
        jax.ShapeDtypeStruct((b, t, GW), F32),
        jax.ShapeDtypeStruct((b, t, GW), BF16),
        jax.ShapeDtypeStruct((b, t, QK_ROPE), F32),
        jax.ShapeDtypeStruct((b, t, ROPE_LANES), BF16),
        jax.ShapeDtypeStruct((b, CONV_WIDTH - 1, GW), F32),
        jax.ShapeDtypeStruct((b, 1, GW), F32),
        jax.ShapeDtypeStruct((b, POOL_BUF, GW), F32),
    ]
    return pl.pallas_call(
        functools.partial(_premix_prompt_kernel, tm=tm),
        grid=grid, in_specs=in_specs, out_specs=out_specs, out_shape=out_shape,
        scratch_shapes=[pltpu.VMEM((tm + 8, GW), F32), pltpu.VMEM((tm + 16, GW), F32),
                        pltpu.VMEM((1, GW), F32)],
        compiler_params=_cparams(("arbitrary", "arbitrary")),
        name="premix_prompt",
    )(x, mod, lw["w_in"], lw["sgu_w"], lw["sgu_bias"], lw["vec"], lw["w_gate"], lw["b_gate"],
      lw["w_q"], lw["w_uk"], lw["w_pool"], cos, sin)


def _attn_prompt_kernel(ql_ref, qr_ref, k_ref, kr_ref, wuv_ref, o_ref, m_ref, l_ref, acc_ref, *, tq):
    qi = pl.program_id(1)
    rows = N_HEADS * tq
    ql = ql_ref[0].reshape(rows, GW)
    qr = qr_ref[0].reshape(rows, ROPE_LANES)
    m_ref[...] = jnp.full((rows, 1), -jnp.inf, F32)
    l_ref[...] = jnp.zeros((rows, 1), F32)
    acc_ref[...] = jnp.zeros((rows, GW), F32)

    def step(j, masked):
        start = pl.multiple_of(j * tq, tq)
        k = k_ref[0, pl.ds(start, tq), :]
        kr = kr_ref[0, pl.ds(start, tq), :]
        s = _dot_nt(ql, k) + _dot_nt(qr, kr)
        if masked:
            qpos = _mod(lax.broadcasted_iota(jnp.int32, (rows, tq), 0), tq)
            kpos = lax.broadcasted_iota(jnp.int32, (rows, tq), 1)
            s = jnp.where(kpos <= qpos, s, -jnp.inf)
        m_old = m_ref[...]
        m_new = jnp.maximum(m_old, jnp.max(s, -1, keepdims=True))
        alpha = jnp.exp(m_old - m_new)
        p = jnp.exp(s - m_new)
        l_ref[...] = alpha * l_ref[...] + jnp.sum(p, -1, keepdims=True)
        acc_ref[...] = alpha * acc_ref[...] + _dot(p.astype(BF16), k)
        m_ref[...] = m_new

    def body(j, carry):
        step(j, False)
        return carry

    lax.fori_loop(0, qi, body, 0)
    step(qi, True)
    o = acc_ref[...] / l_ref[...]
    out = _dot(o[0:tq].astype(BF16), wuv_ref[0])
    for hh in range(1, N_HEADS):
        out = out + _dot(o[hh * tq:(hh + 1) * tq].astype(BF16), wuv_ref[hh])
    o_ref[0] = out.astype(BF16)


def _attn_prompt(qlat, qrope, ckvb, krb, wuv):
    b, _, t, _ = qlat.shape
    tq = ATTN_TILE
    assert t % tq == 0
    rows = N_HEADS * tq
    return pl.pallas_call(
        functools.partial(_attn_prompt_kernel, tq=tq),
        grid=(b, t // tq),
        in_specs=[pl.BlockSpec((1, N_HEADS, tq, GW), lambda i, j: (i, 0, j, 0)),
                  pl.BlockSpec((1, N_HEADS, tq, ROPE_LANES), lambda i, j: (i, 0, j, 0)),
                  pl.BlockSpec((1, t, GW), lambda i, j: (i, 0, 0)),
                  pl.BlockSpec((1, t, ROPE_LANES), lambda i, j: (i, 0, 0)),
                  _full(wuv.shape)],
        out_specs=pl.BlockSpec((1, tq, GW), lambda i, j: (i, j, 0)),
        out_shape=jax.ShapeDtypeStruct((b, t, GW), BF16),
        scratch_shapes=[pltpu.VMEM((rows, 1), F32), pltpu.VMEM((rows, 1), F32),
                        pltpu.VMEM((rows, GW), F32)],
        compiler_params=_cparams(("arbitrary", "arbitrary")),
        name="attn_prompt",
    )(qlat, qrope, ckvb, krb, wuv)


def _premix_sample_kernel(x_ref, mod_ref, win_ref, sguc_ref, sgub_ref, vec_ref, wgate_ref, bgate_ref,
                          wq_ref, wuk_ref, wpool_ref, cos_ref, sin_ref, cbuf_ref, h0_ref, pbuf_ref,
                          abd_ref, qlat_ref, qrope_ref, ckvf_ref, ckvb_ref, krf_ref, krb_ref, vrows_ref,
                          nconv_ref, nh_ref, npool_ref, *, steps, start):
    def vec(i):
        return vec_ref[i:i + 1, :]

    sh1 = mod_ref[:, 0:D_MODEL]
    sc1 = mod_ref[:, D_MODEL:2 * D_MODEL]
    win = win_ref[...]
    projs = [_dot((_ln(x_ref[t]) * (1.0 + sc1) + sh1).astype(BF16), win) for t in range(steps)]

    vgs = []
    for t in range(steps):
        vg = _ln(_gelu(projs[t][:, OFF_V:OFF_V + GW])) * vec(V_SGU_G) + vec(V_SGU_B)
        vrows_ref[t] = vg
        vgs.append(vg)
    for t in range(steps):
        mixed = sgub_ref[t:t + 1, :]
        for s in range(t + 1):
            mixed = mixed + sguc_ref[t, s:s + 1, :] * vgs[s]
        abd_ref[t, :, 0:GW] = (_gelu(projs[t][:, OFF_U:OFF_U + GW]) * mixed).astype(BF16)

    z = [cbuf_ref[k] for k in range(CONV_WIDTH - 1)] + [p[:, OFF_XB:OFF_XB + GW] for p in projs]
    hstate = h0_ref[...]
    for t in range(steps):
        xc = vec(V_CONV_B) + z[t] * vec(V_CONV_W)
        for k in range(1, CONV_WIDTH):
            xc = xc + z[t + k] * vec(V_CONV_W + k)
        a, bt = _lru_gates(xc, wgate_ref[...], bgate_ref[...], vec(V_LAM))
        hstate = a * hstate + bt
        abd_ref[t, :, GW:2 * GW] = (hstate * _gelu(projs[t][:, OFF_GB:OFF_GB + GW])).astype(BF16)
    nh_ref[...] = hstate
    for k in range(CONV_WIDTH - 1):
        nconv_ref[k] = z[steps + k]

    for t in range(steps):
        cos = cos_ref[t:t + 1, :]
        sin = sin_ref[t:t + 1, :]
        p = projs[t]
        qlat, qr = _mla_queries(p[:, OFF_CQ:OFF_CQ + GW], vec(V_QNORM), wq_ref[...], wuk_ref[...], cos, sin)
        qlat_ref[t] = qlat.astype(BF16)
        qrope_ref[t] = qr.astype(BF16)
        ckv = _rms(p[:, OFF_CKV:OFF_CKV + GW]) * vec(V_KVNORM)
        ckvf_ref[t] = ckv
        ckvb_ref[t] = ckv.astype(BF16)
        kr = (p[:, OFF_KRA:OFF_KRA + ROPE_LANES] * cos + p[:, OFF_KRB:OFF_KRB + ROPE_LANES] * sin)[:, 0:QK_ROPE]
        krf_ref[t] = kr
        krb_ref[t] = kr.astype(BF16)

    zp = [pbuf_ref[k] for k in range(POOL_BUF)] + [p[:, OFF_PD:OFF_PD + GW] for p in projs]
    wins = _pool_windows()
    for t in range(steps):
        sums = []
        run = zp[POOL_BUF + t]
        for j in range(1, POOL_WINDOWS[-1]):
            if j in POOL_WINDOWS:
                sums.append(run)
            run = run + zp[POOL_BUF + t - j]
        sums.append(run)
        cnt = jnp.minimum(float(start + t + 1), wins)
        d = (_pool_select(*sums) / cnt - zp[POOL_BUF + t]).astype(BF16)
        abd_ref[t, :, 2 * GW:3 * GW] = (_dot(d, wpool_ref[...]) * vec(V_PSCALE)).astype(BF16)
    for k in range(POOL_BUF):
        npool_ref[k] = zp[steps + k]


def _premix_sample(x_tm, mod, lw, sgu_coef, sgu_bias, cos, sin, cbuf, h0, pbuf, start):
    steps, bd, d = x_tm.shape
    assert steps <= min(CHUNK, NEW_KEY_PAD)
    f = lambda w, dt: jax.ShapeDtypeStruct((steps, bd, w), dt)
    out_shape = [f(3 * GW, BF16), f(N_HEADS * GW, BF16), f(ROPE_LANES, BF16), f(GW, F32), f(GW, BF16),
                 f(QK_ROPE, F32), f(QK_ROPE, BF16), f(GW, F32),
                 jax.ShapeDtypeStruct((CONV_WIDTH - 1, bd, GW), F32),
                 jax.ShapeDtypeStruct((bd, GW), F32),
                 jax.ShapeDtypeStruct((POOL_BUF, bd, GW), F32)]
    args = (x_tm, mod, lw["w_in"], sgu_coef, sgu_bias, lw["vec"], lw["w_gate"], lw["b_gate"],
            lw["w_q"], lw["w_uk"], lw["w_pool"], cos, sin, cbuf, h0, pbuf)
    in_specs = [_full(a.shape) for a in args]
    in_specs[1] = pl.BlockSpec((bd, 2 * d), lambda i: (0, 0))
    return pl.pallas_call(
        functools.partial(_premix_sample_kernel, steps=steps, start=start),
        grid=(1,), in_specs=in_specs,
        out_specs=[_full(s.shape) for s in out_shape], out_shape=out_shape,
        compiler_params=_cparams(("arbitrary",)),
        name="premix_sample",
    )(*args)


def _decode_kernel(pt_ref, ql_ref, qr_ref, nk_ref, nkr_ref, *rest, pages, steps):
    del pt_ref
    ckv_pages = rest[:pages]
    kr_pages = rest[pages:2 * pages]
    o_ref = rest[2 * pages]
    m_ref, l_ref, acc_ref = rest[2 * pages + 1:]
    j = pl.program_id(1)
    rows = N_HEADS * steps

    @pl.when(j == 0)
    def _():
        m_ref[...] = jnp.full((rows, 1), -jnp.inf, F32)
        l_ref[...] = jnp.zeros((rows, 1), F32)
        acc_ref[...] = jnp.zeros((rows, GW), F32)

    ql = ql_ref[0]
    qr = qr_ref[0]

    def update(s, values):
        m_old = m_ref[...]
        m_new = jnp.maximum(m_old, jnp.max(s, -1, keepdims=True))
        alpha = jnp.exp(m_old - m_new)
        p = jnp.exp(s - m_new)
        l_ref[...] = alpha * l_ref[...] + jnp.sum(p, -1, keepdims=True)
        p = p.astype(BF16)
        pv = None
        lo = 0
        for v in values:
            part = _dot(p[:, lo:lo + v.shape[0]], v)
            pv = part if pv is None else pv + part
            lo += v.shape[0]
        acc_ref[...] = alpha * acc_ref[...] + pv
        m_ref[...] = m_new

    ks = [r[...].astype(BF16) for r in ckv_pages]
    krs = [r[...].astype(BF16) for r in kr_pages]
    s = jnp.concatenate([_dot_nt(ql, k) + _dot_nt(qr, kr) for k, kr in zip(ks, krs)], -1)
    update(s, ks)

    @pl.when(j == pl.num_programs(1) - 1)
    def _():
        nk = nk_ref[0]
        s_new = _dot_nt(ql, nk) + _dot_nt(qr, nkr_ref[0])
        qstep = _mod(lax.broadcasted_iota(jnp.int32, (rows, NEW_KEY_PAD), 0), steps)
        kstep = lax.broadcasted_iota(jnp.int32, (rows, NEW_KEY_PAD), 1)
        update(jnp.where(kstep <= qstep, s_new, -jnp.inf), [nk])
        o_ref[0] = acc_ref[...] / l_ref[...]


def _decode_attn(layer, page_table, cache_ckv, cache_kr, qlat, qrope, nk, nkr):
    bd, rows, _ = qlat.shape
    n_pages = page_table.shape[1]
    pages = min(DECODE_PAGES, n_pages)
    assert n_pages % pages == 0
    steps = rows // N_HEADS

    def page_spec(width, k):
        return pl.BlockSpec((None, None, PAGE_SIZE, width),
                            lambda i, j, pt: (layer, pt[i, j * pages + k], 0, 0))

    per_seq = lambda r, w: pl.BlockSpec((1, r, w), lambda i, j, pt: (i, 0, 0))
    in_specs = ([per_seq(rows, GW), per_seq(rows, QK_ROPE), per_seq(NEW_KEY_PAD, GW), per_seq(NEW_KEY_PAD, QK_ROPE)]
                + [page_spec(GW, k) for k in range(pages)]
                + [page_spec(QK_ROPE, k) for k in range(pages)])
    grid_spec = pltpu.PrefetchScalarGridSpec(
        num_scalar_prefetch=1, grid=(bd, n_pages // pages),
        in_specs=in_specs, out_specs=per_seq(rows, GW),
        scratch_shapes=[pltpu.VMEM((rows, 1), F32), pltpu.VMEM((rows, 1), F32), pltpu.VMEM((rows, GW), F32)])
    return pl.pallas_call(
        functools.partial(_decode_kernel, pages=pages, steps=steps),
        grid_spec=grid_spec,
        out_shape=jax.ShapeDtypeStruct((bd, rows, GW), F32),
        compiler_params=_cparams(("arbitrary", "arbitrary")),
        name="decode_attn",
    )(page_table, qlat, qrope, nk, nkr, *([cache_ckv] * pages), *([cache_kr] * pages))


def _uv_kernel(o_ref, wuv_ref, out_ref):
    out = _dot(o_ref[0].astype(BF16), wuv_ref[0])
    for hh in range(1, N_HEADS):
        out = out + _dot(o_ref[hh].astype(BF16), wuv_ref[hh])
    out_ref[...] = out.astype(BF16)


def _uv_project(o_heads, wuv):
    _, n, _ = o_heads.shape
    return pl.pallas_call(
        _uv_kernel, grid=(1,),
        in_specs=[_full(o_heads.shape), _full(wuv.shape)],
        out_specs=_full((n, GW)), out_shape=jax.ShapeDtypeStruct((n, GW), BF16),
        compiler_params=_cparams(("arbitrary",)), name="uv_project",
    )(o_heads, wuv)


def _mix_norm(x_ref, abd_ref, c_ref, moda_ref, modb_ref, wabd_ref, wc_ref, lnv_ref):
    mix = _dot(abd_ref[...], wabd_ref[...]) + _dot(c_ref[...], wc_ref[...])
    g1 = moda_ref[0, :, 0:D_MODEL]
    sh2 = moda_ref[0, :, D_MODEL:2 * D_MODEL]
    sc2 = modb_ref[0, :, 0:D_MODEL]
    x1 = _ln(ALPHA * x_ref[...] + g1 * mix) * lnv_ref[0:1, :] + lnv_ref[1:2, :]
    h2 = (_ln(x1) * (1.0 + sc2) + sh2).astype(BF16)
    return x1, h2


def _swiglu(h2, w1, w3, w2):
    return _dot((_silu(_dot(h2, w1)) * _dot(h2, w3)).astype(BF16), w2)


def _post_dense_kernel(x_ref, abd_ref, c_ref, moda_ref, modb_ref, wabd_ref, wc_ref, lnv_ref,
                       w1_ref, w3_ref, w2_ref, o_ref):
    x1, h2 = _mix_norm(x_ref, abd_ref, c_ref, moda_ref, modb_ref, wabd_ref, wc_ref, lnv_ref)
    f = _swiglu(h2, w1_ref[...], w3_ref[...], w2_ref[...])
    g2 = modb_ref[0, :, D_MODEL:2 * D_MODEL]
    o_ref[...] = _ln(ALPHA * x1 + g2 * f) * lnv_ref[2:3, :] + lnv_ref[3:4, :]


def _token_specs(tm, tiles_per_group, mod_rows):
    mi = lambda blk: (lambda i, *_: (i // tiles_per_group, 0, blk))
    tok = lambda w: pl.BlockSpec((tm, w), lambda i, *_: (i, 0))
    mod = lambda blk: pl.BlockSpec((1, mod_rows, 2 * D_MODEL), mi(blk))
    return tok, mod


def _post_dense(x, abd, c, mod, lw, fw, tm, tiles_per_group):
    n, d = x.shape
    tok, modspec = _token_specs(tm, tiles_per_group, mod.shape[1])
    weights = (lw["w_abd"], lw["w_c"], lw["lnv"], fw["w1"], fw["w3"], fw["w2"])
    return pl.pallas_call(
        _post_dense_kernel, grid=(n // tm,),
        in_specs=[tok(d), tok(3 * GW), tok(GW), modspec(1), modspec(2)] + [_full(w.shape) for w in weights],
        out_specs=tok(d), out_shape=jax.ShapeDtypeStruct((n, d), F32),
        compiler_params=_cparams(("arbitrary",)), name="post_dense",
    )(x, abd, c, mod, mod, *weights)


def _post_route_kernel(x_ref, abd_ref, c_ref, moda_ref, modb_ref, wabd_ref, wc_ref, lnv_ref,
                       wr_ref, br_ref, x1_ref, h2_ref, dg_ref):
    x1, h2 = _mix_norm(x_ref, abd_ref, c_ref, moda_ref, modb_ref, wabd_ref, wc_ref, lnv_ref)
    x1_ref[...] = x1
    h2_ref[...] = h2
    lane = lax.broadcasted_iota(jnp.int32, (1, 128), 1)
    logits = jnp.where(lane < N_EXPERTS, _dot(h2, wr_ref[...]) + br_ref[...], -jnp.inf)
    m1 = jnp.max(logits, -1, keepdims=True)
    i1 = jnp.min(jnp.where(logits == m1, lane, 128), -1, keepdims=True)
    rest = jnp.where(lane == i1, -jnp.inf, logits)
    m2 = jnp.max(rest, -1, keepdims=True)
    i2 = jnp.min(jnp.where(rest == m2, lane, 128), -1, keepdims=True)
    e2 = jnp.exp(m2 - m1)
    gate1 = 1.0 / (1.0 + e2)
    gate2 = e2 / (1.0 + e2)
    dg_ref[...] = jnp.where(lane == i1, gate1, 0.0) + jnp.where(lane == i2, gate2, 0.0)


def _post_route(x, abd, c, mod, lw, mw, tm, tiles_per_group):
    n, d = x.shape
    tok, modspec = _token_specs(tm, tiles_per_group, mod.shape[1])
    weights = (lw["w_abd"], lw["w_c"], lw["lnv"], mw["w_router"], mw["b_router"])
    return pl.pallas_call(
        _post_route_kernel, grid=(n // tm,),
        in_specs=[tok(d), tok(3 * GW), tok(GW), modspec(1), modspec(2)] + [_full(w.shape) for w in weights],
        out_specs=[tok(d), tok(d), tok(128)],
        out_shape=[jax.ShapeDtypeStruct((n, d), F32), jax.ShapeDtypeStruct((n, d), BF16),
                   jax.ShapeDtypeStruct((n, 128), F32)],
        compiler_params=_cparams(("arbitrary",)), name="post_route",
    )(x, abd, c, mod, mod, *weights)


def _moe_dense_kernel(x1_ref, h2_ref, dg_ref, modb_ref, lnv_ref, w1_ref, w3_ref, w2_ref, o_ref, acc_ref):
    e = pl.program_id(1)

    @pl.when(e == 0)
    def _():
        acc_ref[...] = jnp.zeros_like(acc_ref)

    lane = lax.broadcasted_iota(jnp.int32, (1, 128), 1)
    gate = jnp.sum(jnp.where(lane == e, dg_ref[...], 0.0), -1, keepdims=True)
    acc_ref[...] += gate * _swiglu(h2_ref[...], w1_ref[...], w3_ref[...], w2_ref[...])

    @pl.when(e == pl.num_programs(1) - 1)
    def _():
        g2 = modb_ref[0, :, D_MODEL:2 * D_MODEL]
        o_ref[...] = _ln(ALPHA * x1_ref[...] + g2 * acc_ref[...]) * lnv_ref[2:3, :] + lnv_ref[3:4, :]


def _moe_dense(x1, h2, dg, mod, lw, mw, tm, tiles_per_group):
    n, d = x1.shape
    tok, modspec = _token_specs(tm, tiles_per_group, mod.shape[1])
    wspec = lambda shape: pl.BlockSpec((None,) + shape, lambda i, e: (e, 0, 0))
    return pl.pallas_call(
        _moe_dense_kernel, grid=(n // tm, N_EXPERTS),
        in_specs=[tok(d), tok(d), tok(128), modspec(2), _full(lw["lnv"].shape),
                  wspec((d, D_FF)), wspec((d, D_FF)), wspec((D_FF, d))],
        out_specs=tok(d), out_shape=jax.ShapeDtypeStruct((n, d), F32),
        scratch_shapes=[pltpu.VMEM((tm, d), F32)],
        compiler_params=_cparams(("arbitrary", "arbitrary")), name="moe_dense",
    )(x1, h2, dg, mod, lw["lnv"], mw["w1"], mw["w3"], mw["w2"])


def _block_diag(blocks):
    n, r, c = blocks.shape
    out = jnp.zeros((n * r, n * c), blocks.dtype)
    for i in range(n):
        out = out.at[i * r:(i + 1) * r, i * c:(i + 1) * c].set(blocks[i])
    return out


def _swap_rope_halves(w):
    half = QK_ROPE // 2
    return jnp.concatenate([w[..., half:], w[..., :half]], -1)


def _layer_weights(l, p):
    w_in = p["w_in"][l]
    s = np.cumsum([GW, GW, GW, GW, GW, GW, QK_ROPE])
    kr_cols = w_in[:, s[5]:s[6]]
    w_in_r = jnp.concatenate(
        [w_in[:, :s[5]], w_in[:, s[6]:], jnp.tile(kr_cols, (1, N_HEADS)),
         jnp.tile(_swap_rope_halves(kr_cols), (1, N_HEADS))], 1).astype(BF16)
    vec = jnp.zeros((VEC_ROWS, GW), F32)
    for i, name in ((V_SGU_G, "sgu_ln_g"), (V_SGU_B, "sgu_ln_b"), (V_CONV_B, "lru_conv_b"), (V_LAM, "lru_lambda"),
                    (V_QNORM, "mla_q_norm"), (V_KVNORM, "mla_kv_norm"), (V_PSCALE, "pool_scale")):
        vec = vec.at[i].set(p[name][l])
    vec = vec.at[V_CONV_W:V_CONV_W + CONV_WIDTH].set(p["lru_conv_w"][l])
    w_uq = p["mla_w_uq"][l]
    q_rope_cols = w_uq[:, :, QK_NOPE:]
    w_q = jnp.concatenate([w_uq[:, :, :QK_NOPE].reshape(GW, GW), q_rope_cols.reshape(GW, ROPE_LANES),
                           _swap_rope_halves(q_rope_cols).reshape(GW, ROPE_LANES)], 1).astype(BF16)
    w_uk = _block_diag(jnp.transpose(p["mla_w_uk"][l], (1, 2, 0))).astype(BF16)
    w_uv = p["mla_w_uv"][l]
    w_uv_pad = jnp.zeros((N_HEADS, GW, GW), F32)
    for hh in range(N_HEADS):
        w_uv_pad = w_uv_pad.at[hh, :, hh * HEAD_DIM:(hh + 1) * HEAD_DIM].set(w_uv[:, hh, :])
    w_out = p["w_out"][l]
    return {
        "w_in": w_in_r,
        "sgu_w": p["sgu_w"][l],
        "sgu_b": p["sgu_b"][l],
        "vec": vec,
        "w_gate": jnp.concatenate([_block_diag(p["lru_wa"][l]), _block_diag(p["lru_wx"][l])], 1).astype(BF16),
        "b_gate": jnp.concatenate([p["lru_ba"][l], p["lru_bx"][l]])[None, :],
        "w_q": w_q,
        "w_uk": w_uk,
        "w_uv": w_uv_pad.astype(BF16),
        "w_pool": _block_diag(p["pool_w"][l]).astype(BF16),
        "w_abd": jnp.concatenate([w_out[:2 * GW], w_out[3 * GW:]], 0).astype(BF16),
        "w_c": w_out[2 * GW:3 * GW].astype(BF16),
        "lnv": jnp.stack([p["ln1_g"][l], p["ln1_b"][l], p["ln2_g"][l], p["ln2_b"][l]]),
    }


def _rope_tables(start, steps):
    half = QK_ROPE // 2
    freq = ROPE_THETA ** (-jnp.arange(half, dtype=F32) / half)
    ang = (start + jnp.arange(steps)).astype(F32)[:, None] * freq[None, :]
    cos = jnp.cos(ang)
    sin = jnp.sin(ang)
    return (jnp.tile(jnp.concatenate([cos, cos], -1), (1, N_HEADS)),
            jnp.tile(jnp.concatenate([-sin, sin], -1), (1, N_HEADS)))


def kernel(x_prompt, x_sample, c_prompt, c_sample, cache_ckv, cache_kr, page_table, state_lru_conv, state_lru_h, state_pool, ada_w, ada_b, w_in, w_out, ln1_g, ln1_b, ln2_g, ln2_b, sgu_ln_g, sgu_ln_b, sgu_w, sgu_b, lru_conv_w, lru_conv_b, lru_wa, lru_ba, lru_wx, lru_bx, lru_lambda, mla_q_norm, mla_w_uq, mla_kv_norm, mla_w_uk, mla_w_uv, pool_w, pool_scale, ffn_w1, ffn_w3, ffn_w2, moe_router, moe_router_b, moe_w1, moe_w3, moe_w2):
    p = dict(w_in=w_in, w_out=w_out, ln1_g=ln1_g, ln1_b=ln1_b, ln2_g=ln2_g, ln2_b=ln2_b,
             sgu_ln_g=sgu_ln_g, sgu_ln_b=sgu_ln_b, sgu_w=sgu_w, sgu_b=sgu_b, lru_conv_w=lru_conv_w,
             lru_conv_b=lru_conv_b, lru_wa=lru_wa, lru_ba=lru_ba, lru_wx=lru_wx, lru_bx=lru_bx,
             lru_lambda=lru_lambda, mla_q_norm=mla_q_norm, mla_w_uq=mla_w_uq, mla_kv_norm=mla_kv_norm,
             mla_w_uk=mla_w_uk, mla_w_uv=mla_w_uv, pool_w=pool_w, pool_scale=pool_scale)
    bp, tp, d = x_prompt.shape
    bd, td, _ = x_sample.shape
    n_pages = page_table.shape[1]
    past_len = n_pages * PAGE_SIZE
    tmp = PROMPT_TILE

    cos_p, sin_p = _rope_tables(0, tp)
    cos_s, sin_s = _rope_tables(past_len, td)
    pad = (-(bd + bp)) % 8
    c_all = jnp.concatenate([c_sample, c_prompt, jnp.zeros((pad, d), F32)], 0)

    xp = x_prompt
    xs = jnp.transpose(x_sample, (1, 0, 2))
    sp = [[] for _ in range(5)]
    ss = [[] for _ in range(6)]
    for l in range(DEPTH):
        lw = _layer_weights(l, p)
        mod = _ada(c_all, ada_w[l].astype(BF16), ada_b[l][None, :])
        mod_s = mod[:bd]
        mod_p = mod[bd:bd + bp][:, None, :]

        (abd_p, qlat_p, qrope_p, ckv_p, ckvb_p, kr_p, krb_p, nconv_p, nh_p, npool_p) = _premix_prompt(
            xp, mod_p, dict(lw, sgu_w=lw["sgu_w"][:, :CHUNK, :CHUNK],
                            sgu_bias=jnp.repeat(lw["sgu_b"][:, :CHUNK].T, HEAD_DIM, axis=1)), cos_p, sin_p)
        c_p = _attn_prompt(qlat_p, qrope_p, ckvb_p, krb_p, lw["w_uv"])

        sgu_coef = jnp.repeat(jnp.transpose(lw["sgu_w"][:, :td, :td], (1, 2, 0)), HEAD_DIM, axis=2)
        sgu_bias_s = jnp.repeat(lw["sgu_b"][:, :td].T, HEAD_DIM, axis=1)
        (abd_s, qlat_s, qrope_s, ckv_s, ckvb_s, kr_s, krb_s, vrows_s, nconv_s, nh_s, npool_s) = _premix_sample(
            xs, mod_s, lw, sgu_coef, sgu_bias_s, cos_s, sin_s,
            jnp.transpose(state_lru_conv[l], (1, 0, 2)), state_lru_h[l],
            jnp.transpose(state_pool[l], (1, 0, 2)), past_len)
        to_rows = lambda a, w: jnp.transpose(a.reshape(td, bd, N_HEADS, w), (1, 2, 0, 3)).reshape(bd, N_HEADS * td, w)
        pad_keys = lambda a: jnp.pad(jnp.transpose(a, (1, 0, 2)), ((0, 0), (0, NEW_KEY_PAD - td), (0, 0)))
        o_s = _decode_attn(l, page_table, cache_ckv, cache_kr, to_rows(qlat_s, GW), to_rows(qrope_s, QK_ROPE),
                           pad_keys(ckvb_s), pad_keys(krb_s))
        o_heads = jnp.transpose(o_s.reshape(bd, N_HEADS, td, GW), (1, 2, 0, 3)).reshape(N_HEADS, td * bd, GW)
        c_s = _uv_project(o_heads, lw["w_uv"])

        xp2 = xp.reshape(bp * tp, d)
        xs2 = xs.reshape(td * bd, d)
        args_p = (xp2, abd_p.reshape(bp * tp, 3 * GW), c_p.reshape(bp * tp, GW), mod_p)
        args_s = (xs2, abd_s.reshape(td * bd, 3 * GW), c_s, mod_s[None])
        j = l // 2
        if l % 2 == 0:
            fw = {"w1": ffn_w1[j].astype(BF16), "w3": ffn_w3[j].astype(BF16), "w2": ffn_w2[j].astype(BF16)}
            xp2 = _post_dense(*args_p, lw, fw, tmp, tp // tmp)
            xs2 = _post_dense(*args_s, lw, fw, bd, td)
        else:
            mw = {"w_router": jnp.pad(moe_router[j], ((0, 0), (0, 128 - N_EXPERTS))).astype(BF16),
                  "b_router": jnp.pad(moe_router_b[j], (0, 128 - N_EXPERTS))[None, :],
                  "w1": moe_w1[j].astype(BF16), "w3": moe_w3[j].astype(BF16), "w2": moe_w2[j].astype(BF16)}
            x1p, h2p, dgp = _post_route(*args_p, lw, mw, tmp, tp // tmp)
            x1s, h2s, dgs = _post_route(*args_s, lw, mw, bd, td)
            xp2 = _moe_dense(x1p, h2p, dgp, mod_p, lw, mw, tmp, tp // tmp)
            xs2 = _moe_dense(x1s, h2s, dgs, mod_s[None], lw, mw, bd, td)
        xp = xp2.reshape(bp, tp, d)
        xs = xs2.reshape(td, bd, d)

        for k, v in enumerate((ckv_p, kr_p, nconv_p, nh_p[:, 0], npool_p)):
            sp[k].append(v)
        bt = lambda a: jnp.transpose(a, (1, 0, 2))
        for k, v in enumerate((bt(ckv_s), bt(kr_s), bt(nconv_s), nh_s, bt(npool_s), bt(vrows_s))):
            ss[k].append(v)

    return (xp, jnp.transpose(xs, (1, 0, 2)),
            jnp.stack(sp[0]), jnp.stack(sp[1]), jnp.stack(sp[2]), jnp.stack(sp[3]), jnp.stack(sp[4]),
            jnp.stack(ss[0]), jnp.stack(ss[1]), jnp.stack(ss[2]), jnp.stack(ss[3]), jnp.stack(ss[4]),
            jnp.stack(ss[5]))
```

```python
import functools

import jax
import jax.numpy as jnp
import numpy as np
from jax import lax
from jax.experimental import pallas as pl
from jax.experimental.pallas import tpu as pltpu

F32 = jnp.float32
BF16 = jnp.bfloat16

D_MODEL = 1024
DEPTH = 2
PAGE_SIZE = 128
GROUP_WIDTH = D_MODEL // 4
HEAD_DIM = 64
N_HEADS = GROUP_WIDTH // HEAD_DIM
CHUNK = 128
CONV_WIDTH = 4
LRU_C = 8.0
QK_NOPE = HEAD_DIM
QK_ROPE = HEAD_DIM // 2
MLA_SCALE = (QK_NOPE + QK_ROPE) ** -0.5
ROPE_THETA = 10000.0
POOL_WINDOWS = (2, 4, 8, 16)
POOL_BUF = 15
POOL_GC = GROUP_WIDTH // 4
D_FF = ((8 * D_MODEL // 3 + 127) // 128) * 128
N_EXPERTS = 8
ALPHA = (2 * DEPTH) ** 0.25
LN_EPS = 1e-5
RMS_EPS = 1e-6

GW = GROUP_WIDTH
OFF_U, OFF_V, OFF_XB, OFF_GB, OFF_CQ, OFF_CKV, OFF_PD, OFF_KRA, OFF_KRB = (
    0, GW, 2 * GW, 3 * GW, 4 * GW, 5 * GW, 6 * GW, 7 * GW, 7 * GW + 128)
P_IN_PAD = 8 * GW
ROPE_LANES = N_HEADS * QK_ROPE
HEAD_LANES = 128

V_SGU_G, V_SGU_B, V_CONV_B, V_LAM, V_QNORM, V_KVNORM, V_PSCALE = 0, 1, 2, 3, 4, 5, 6
V_CONV_W = 8
VEC_ROWS = 16

VMEM_LIMIT_BYTES = 56 * 1024 * 1024
PROMPT_TILE = 256
ATTN_TILE = 256
DECODE_PAGES = 16
NEW_KEY_PAD = 16


def _dot(a, b):
    return jnp.dot(a, b, preferred_element_type=F32)


def _dot_nt(a, b):
    return lax.dot_general(a, b, (((1,), (1,)), ((), ())), preferred_element_type=F32)


def _ln(x):
    mu = jnp.mean(x, -1, keepdims=True)
    xc = x - mu
    var = jnp.mean(xc * xc, -1, keepdims=True)
    return xc * lax.rsqrt(var + LN_EPS)


def _rms(x):
    return x * lax.rsqrt(jnp.mean(x * x, -1, keepdims=True) + RMS_EPS)


def _gelu(x):
    return 0.5 * x * (1.0 + jnp.tanh(0.7978845608028654 * (x + 0.044715 * (x * x * x))))


def _silu(x):
    return x * jax.nn.sigmoid(x)


def _softplus(x):
    return jnp.maximum(x, 0.0) + jnp.log1p(jnp.exp(-jnp.abs(x)))


def _lane_group(width, group):
    assert group & (group - 1) == 0
    return lax.broadcasted_iota(jnp.int32, (1, width), 1) >> (group.bit_length() - 1)


def _mod(x, n):
    return x & (n - 1) if n & (n - 1) == 0 else lax.rem(x, n)


def _cparams(sem):
    return pltpu.CompilerParams(dimension_semantics=sem, vmem_limit_bytes=VMEM_LIMIT_BYTES)


def _full(shape):
    n = len(shape)
    return pl.BlockSpec(shape, lambda *_: (0,) * n)


def _ada_kernel(c_ref, w_ref, b_ref, o_ref):
    o_ref[...] = _dot(_silu(c_ref[...]).astype(BF16), w_ref[...]) + b_ref[...]


def _ada(c_all, w, b):
    m, d = c_all.shape
    n = w.shape[1]
    tn = 512
    return pl.pallas_call(
        _ada_kernel,
        grid=(n // tn,),
        in_specs=[pl.BlockSpec((m, d), lambda j: (0, 0)),
                  pl.BlockSpec((d, tn), lambda j: (0, j)),
                  pl.BlockSpec((1, tn), lambda j: (0, j))],
        out_specs=pl.BlockSpec((m, tn), lambda j: (0, j)),
        out_shape=jax.ShapeDtypeStruct((m, n), F32),
        compiler_params=_cparams(("arbitrary",)),
        name="ada",
    )(c_all, w, b)


def _lru_gates(xc, wgate, bgate, lam):
    g = _dot(xc.astype(BF16), wgate) + bgate
    r = jax.nn.sigmoid(g[:, :GW])
    i = jax.nn.sigmoid(g[:, GW:])
    log_a = (-LRU_C) * r * _softplus(-lam)
    a = jnp.exp(log_a)
    bt = jnp.sqrt(-jnp.tanh(log_a) * (a * a + 1.0)) * i * xc
    return a, bt


def _mla_queries(cq, qnorm, wq, wuk, cos, sin):
    q = _dot((_rms(cq) * qnorm).astype(BF16), wq)
    qlat = _dot(q[:, :GW].astype(BF16), wuk) * MLA_SCALE
    qr = (q[:, GW:GW + ROPE_LANES] * cos + q[:, GW + ROPE_LANES:] * sin) * MLA_SCALE
    return qlat, qr


def _pool_select(s2, s4, s8, s16):
    grp = _lane_group(GW, POOL_GC)
    return jnp.where(grp == 0, s2, jnp.where(grp == 1, s4, jnp.where(grp == 2, s8, s16)))


def _pool_windows():
    grp = _lane_group(GW, POOL_GC)
    w = jnp.where(grp == 0, POOL_WINDOWS[0],
                  jnp.where(grp == 1, POOL_WINDOWS[1],
                            jnp.where(grp == 2, POOL_WINDOWS[2], POOL_WINDOWS[3])))
    return w.astype(F32)


def _premix_prompt_kernel(x_ref, mod_ref, win_ref, sguw_ref, sgub_ref, vec_ref, wgate_ref, bgate_ref,
                          wq_ref, wkv_ref, wpool_ref, rope_ref,
                          abd_ref, qcat_ref, kcat_ref, vt_ref, ckvf_ref, krf_ref,
                          nconv_ref, nh_ref, npool_ref,
                          zc_ref, zp_ref, hc_ref, *, tm):
    t = pl.program_id(1)

    @pl.when(t == 0)
    def _():
        zc_ref[0:8, :] = jnp.zeros((8, GW), F32)
        zp_ref[0:16, :] = jnp.zeros((16, GW), F32)
        hc_ref[...] = jnp.zeros((1, GW), F32)

    def vec(i):
        return vec_ref[i:i + 1, :]

    x = x_ref[0]
    sh1 = mod_ref[0, :, 0:D_MODEL]
    sc1 = mod_ref[0, :, D_MODEL:2 * D_MODEL]
    h = _ln(x) * (1.0 + sc1) + sh1
    proj = _dot(h.astype(BF16), win_ref[...])

    ug = _gelu(proj[:, OFF_U:OFF_U + GW])
    vg = _ln(_gelu(proj[:, OFF_V:OFF_V + GW])) * vec(V_SGU_G) + vec(V_SGU_B)
    vgb = vg.astype(BF16)
    head = _lane_group(GW, HEAD_DIM)
    row = lax.broadcasted_iota(jnp.int32, (CHUNK, CHUNK), 0)
    col = lax.broadcasted_iota(jnp.int32, (CHUNK, CHUNK), 1)
    wmix = [jnp.where(col <= row, sguw_ref[hh], 0.0).astype(BF16) for hh in range(N_HEADS)]
    parts = []
    for c in range(tm // CHUNK):
        vc = vgb[c * CHUNK:(c + 1) * CHUNK]
        acc = _dot(wmix[0], vc)
        for hh in range(1, N_HEADS):
            acc = jnp.where(head == hh, _dot(wmix[hh], vc), acc)
        parts.append(acc + sgub_ref[...])
    mixed = jnp.concatenate(parts, 0) if len(parts) > 1 else parts[0]
    abd_ref[0, :, 0:GW] = (ug * mixed).astype(BF16)

    xb = proj[:, OFF_XB:OFF_XB + GW]
    zc_ref[8:8 + tm, :] = xb
    z = zc_ref[...]
    xc = (vec(V_CONV_B) + vec(V_CONV_W + 3) * z + vec(V_CONV_W + 2) * pltpu.roll(z, 1, 0)
          + vec(V_CONV_W + 1) * pltpu.roll(z, 2, 0) + vec(V_CONV_W) * pltpu.roll(z, 3, 0))[8:]
    nconv_ref[0] = zc_ref[pl.ds(8 + tm - (CONV_WIDTH - 1), CONV_WIDTH - 1), :]
    zc_ref[0:8, :] = xb[tm - 8:tm]
    a, bt = _lru_gates(xc, wgate_ref[...], bgate_ref[...], vec(V_LAM))
    rowi = lax.broadcasted_iota(jnp.int32, (tm, GW), 0)
    k = 1
    while k < tm:
        keep = rowi >= k
        bt = jnp.where(keep, a * pltpu.roll(bt, k, 0) + bt, bt)
        a = jnp.where(keep, a * pltpu.roll(a, k, 0), a)
        k *= 2
    hs = a * hc_ref[...] + bt
    hc_ref[...] = hs[tm - 1:tm]
    nh_ref[0] = hs[tm - 1:tm]
    abd_ref[0, :, GW:2 * GW] = (hs * _gelu(proj[:, OFF_GB:OFF_GB + GW])).astype(BF16)

    cosq = rope_ref[:, 0:HEAD_LANES]
    sinq = rope_ref[:, HEAD_LANES:2 * HEAD_LANES]
    cosk = rope_ref[:, 2 * HEAD_LANES:3 * HEAD_LANES]
    sink = rope_ref[:, 3 * HEAD_LANES:4 * HEAD_LANES]
    q = _dot((_rms(proj[:, OFF_CQ:OFF_CQ + GW]) * vec(V_QNORM)).astype(BF16), wq_ref[...])
    half = N_HEADS * HEAD_LANES
    for hh in range(N_HEADS):
        lo = hh * HEAD_LANES
        qh = q[:, lo:lo + HEAD_LANES] * cosq + q[:, half + lo:half + lo + HEAD_LANES] * sinq
        qcat_ref[0, hh] = (qh * MLA_SCALE).astype(BF16)
    ckv = _rms(proj[:, OFF_CKV:OFF_CKV + GW]) * vec(V_KVNORM)
    ckvf_ref[0] = ckv
    kv = _dot(ckv.astype(BF16), wkv_ref[...])
    kr = proj[:, OFF_KRA:OFF_KRA + HEAD_LANES] * cosk + proj[:, OFF_KRB:OFF_KRB + HEAD_LANES] * sink
    krf_ref[0] = kr[:, 0:QK_ROPE]
    kr_hi = jnp.where(lax.broadcasted_iota(jnp.int32, (1, HEAD_LANES), 1) >= QK_NOPE, kr, 0.0)
    for hh in range(N_HEADS):
        kcat_ref[0, hh] = (kv[:, hh * HEAD_LANES:(hh + 1) * HEAD_LANES] + kr_hi).astype(BF16)
    vt_ref[0, 0] = kv[:, half:half + GW].T.astype(BF16)

    pd = proj[:, OFF_PD:OFF_PD + GW]
    zp_ref[16:16 + tm, :] = pd
    zz = zp_ref[...]
    s2 = zz + pltpu.roll(zz, 1, 0)
    s4 = s2 + pltpu.roll(s2, 2, 0)
    s8 = s4 + pltpu.roll(s4, 4, 0)
    s16 = s8 + pltpu.roll(s8, 8, 0)
    wsum = _pool_select(s2, s4, s8, s16)[16:]
    pos = (t * tm + lax.broadcasted_iota(jnp.int32, (tm, 1), 0) + 1).astype(F32)
    cnt = jnp.minimum(pos, _pool_windows())
    d = (wsum / cnt - pd).astype(BF16)
    abd_ref[0, :, 2 * GW:3 * GW] = (_dot(d, wpool_ref[...]) * vec(V_PSCALE)).astype(BF16)
    npool_ref[0] = zp_ref[pl.ds(16 + tm - POOL_BUF, POOL_BUF), :]
    zp_ref[0:16, :] = pd[tm - 16:tm]


def _premix_prompt(x, mod, lw, rope):
    b, t, d = x.shape
    tm = PROMPT_TILE
    assert t % tm == 0 and tm % CHUNK == 0 and tm == ATTN_TILE
    grid = (b, t // tm)
    tok = lambda w: pl.BlockSpec((1, tm, w), lambda i, j: (i, j, 0))
    per_b = lambda r: pl.BlockSpec((1, r, GW), lambda i, j: (i, 0, 0))
    heads = pl.BlockSpec((1, N_HEADS, tm, HEAD_LANES), lambda i, j: (i, 0, j, 0))
    in_specs = [
        tok(d),
        pl.BlockSpec((1, 1, 2 * d), lambda i, j: (i, 0, 0)),
        _full(lw["w_in"].shape), _full(lw["sgu_w"].shape), _full(lw["sgu_bias"].shape),
        _full(lw["vec"].shape), _full(lw["w_gate"].shape), _full(lw["b_gate"].shape),
        _full(lw["w_qh"].shape), _full(lw["w_kv"].shape), _full(lw["w_pool"].shape),
        pl.BlockSpec((tm, 4 * HEAD_LANES), lambda i, j: (j, 0)),
    ]
    out_specs = [
        tok(3 * GW), heads, heads,
        pl.BlockSpec((1, 1, GW, tm), lambda i, j: (i, j, 0, 0)),
        tok(GW), tok(QK_ROPE),
        per_b(CONV_WIDTH - 1), per_b(1), per_b(POOL_BUF),
    ]
    out_shape = [
        jax.ShapeDtypeStruct((b, t, 3 * GW), BF16),
        jax.ShapeDtypeStruct((b, N_HEADS, t, HEAD_LANES), BF16),
        jax.ShapeDtypeStruct((b, N_HEADS, t, HEAD_LANES), BF16),
        jax.ShapeDtypeStruct((b, t // tm, GW, tm), BF16),
        jax.ShapeDtypeStruct((b, t, GW), F32),
        jax.ShapeDtypeStruct((b, t, QK_ROPE), F32),
        jax.ShapeDtypeStruct((b, CONV_WIDTH - 1, GW), F32),
        jax.ShapeDtypeStruct((b, 1, GW), F32),
        jax.ShapeDtypeStruct((b, POOL_BUF, GW), F32),
    ]
    return pl.pallas_call(
        functools.partial(_premix_prompt_kernel, tm=tm),
        grid=grid, in_specs=in_specs, out_specs=out_specs, out_shape=out_shape,
        scratch_shapes=[pltpu.VMEM((tm + 8, GW), F32), pltpu.VMEM((tm + 16, GW), F32),
                        pltpu.VMEM((1, GW), F32)],
        compiler_params=_cparams(("arbitrary", "arbitrary")),
        name="premix_prompt",
    )(x, mod, lw["w_in"], lw["sgu_w"], lw["sgu_bias"], lw["vec"], lw["w_gate"], lw["b_gate"],
      lw["w_qh"], lw["w_kv"], lw["w_pool"], rope)


def _attn_prompt_kernel(q_ref, k_ref, vt_ref, o_ref, m_ref, l_ref, acc_ref, *, tq):
    qi = pl.program_id(1)
    cols = N_HEADS * tq
    m_ref[...] = jnp.full((1, cols), -jnp.inf, F32)
    l_ref[...] = jnp.zeros((1, cols), F32)
    acc_ref[...] = jnp.zeros((GW, tq), F32)
    qs = [q_ref[0, hh] for hh in range(N_HEADS)]

    def step(j, masked):
        start = pl.multiple_of(j * tq, tq)
        st = jnp.concatenate(
            [_dot_nt(k_ref[0, hh, pl.ds(start, tq), :], qs[hh]) for hh in range(N_HEADS)], 1)
        if masked:
            kpos = lax.broadcasted_iota(jnp.int32, (tq, cols), 0)
            qpos = _mod(lax.broadcasted_iota(jnp.int32, (tq, cols), 1), tq)
            st = jnp.where(kpos <= qpos, st, -jnp.inf)
        m_old = m_ref[...]
        m_new = jnp.maximum(m_old, jnp.max(st, 0, keepdims=True))
        alpha = jnp.exp(m_old - m_new)
        p = jnp.exp(st - m_new)
        l_ref[...] = alpha * l_ref[...] + jnp.sum(p, 0, keepdims=True)
        m_ref[...] = m_new
        pb = p.astype(BF16)
        vt = vt_ref[0, j]
        for hh in range(N_HEADS):
            r = slice(hh * HEAD_DIM, (hh + 1) * HEAD_DIM)
            c = slice(hh * tq, (hh + 1) * tq)
            acc_ref[r, :] = alpha[:, c] * acc_ref[r, :] + _dot(vt[r, :], pb[:, c])

    def body(j, carry):
        step(j, False)
        return carry

    lax.fori_loop(0, qi, body, 0)
    step(qi, True)
    linv = 1.0 / l_ref[...]
    out = jnp.concatenate(
        [acc_ref[hh * HEAD_DIM:(hh + 1) * HEAD_DIM, :] * linv[:, hh * tq:(hh + 1) * tq] for hh in range(N_HEADS)], 0)
    o_ref[0] = out.T.astype(BF16)


def _attn_prompt(qcat, kcat, vt):
    b, _, t, _ = qcat.shape
    tq = ATTN_TILE
    assert t % tq == 0
    return pl.pallas_call(
        functools.partial(_attn_prompt_kernel, tq=tq),
        grid=(b, t // tq),
        in_specs=[pl.BlockSpec((1, N_HEADS, tq, HEAD_LANES), lambda i, j: (i, 0, j, 0)),
                  pl.BlockSpec((1, N_HEADS, t, HEAD_LANES), lambda i, j: (i, 0, 0, 0)),
                  pl.BlockSpec((1, t // tq, GW, tq), lambda i, j: (i, 0, 0, 0))],
        out_specs=pl.BlockSpec((1, tq, GW), lambda i, j: (i, j, 0)),
        out_shape=jax.ShapeDtypeStruct((b, t, GW), BF16),
        scratch_shapes=[pltpu.VMEM((1, N_HEADS * tq), F32), pltpu.VMEM((1, N_HEADS * tq), F32),
                        pltpu.VMEM((GW, tq), F32)],
        compiler_params=_cparams(("arbitrary", "arbitrary")),
        name="attn_prompt",
    )(qcat, kcat, vt)


def _premix_sample_kernel(x_ref, mod_ref, win_ref, sguc_ref, sgub_ref, vec_ref, wgate_ref, bgate_ref,
                          wq_ref, wuk_ref, wpool_ref, cos_ref, sin_ref, cbuf_ref, h0_ref, pbuf_ref,
                          abd_ref, qlat_ref, qrope_ref, ckvf_ref, ckvb_ref, krf_ref, krb_ref, vrows_ref,
                          nconv_ref, nh_ref, npool_ref, *, steps, start):
    def vec(i):
        return vec_ref[i:i + 1, :]

    sh1 = mod_ref[:, 0:D_MODEL]
    sc1 = mod_ref[:, D_MODEL:2 * D_MODEL]
    win = win_ref[...]
    projs = [_dot((_ln(x_ref[t]) * (1.0 + sc1) + sh1).astype(BF16), win) for t in range(steps)]

    vgs = []
    for t in range(steps):
        vg = _ln(_gelu(projs[t][:, OFF_V:OFF_V + GW])) * vec(V_SGU_G) + vec(V_SGU_B)
        vrows_ref[t] = vg
        vgs.append(vg)
    for t in range(steps):
        mixed = sgub_ref[t:t + 1, :]
        for s in range(t + 1):
            mixed = mixed + sguc_ref[t, s:s + 1, :] * vgs[s]
        abd_ref[t, :, 0:GW] = (_gelu(projs[t][:, OFF_U:OFF_U + GW]) * mixed).astype(BF16)

    z = [cbuf_ref[k] for k in range(CONV_WIDTH - 1)] + [p[:, OFF_XB:OFF_XB + GW] for p in projs]
    hstate = h0_ref[...]
    for t in range(steps):
        xc = vec(V_CONV_B) + z[t] * vec(V_CONV_W)
        for k in range(1, CONV_WIDTH):
            xc = xc + z[t + k] * vec(V_CONV_W + k)
        a, bt = _lru_gates(xc, wgate_ref[...], bgate_ref[...], vec(V_LAM))
        hstate = a * hstate + bt
        abd_ref[t, :, GW:2 * GW] = (hstate * _gelu(projs[t][:, OFF_GB:OFF_GB + GW])).astype(BF16)
    nh_ref[...] = hstate
    for k in range(CONV_WIDTH - 1):
        nconv_ref[k] = z[steps + k]

    for t in range(steps):
        cos = cos_ref[t:t + 1, :]
        sin = sin_ref[t:t + 1, :]
        p = projs[t]
        qlat, qr = _mla_queries(p[:, OFF_CQ:OFF_CQ + GW], vec(V_QNORM), wq_ref[...], wuk_ref[...], cos, sin)
        qlat_ref[t] = qlat.astype(BF16)
        qrope_ref[t] = qr.astype(BF16)
        ckv = _rms(p[:, OFF_CKV:OFF_CKV + GW]) * vec(V_KVNORM)
        ckvf_ref[t] = ckv
        ckvb_ref[t] = ckv.astype(BF16)
        kr = (p[:, OFF_KRA:OFF_KRA + ROPE_LANES] * cos + p[:, OFF_KRB:OFF_KRB + ROPE_LANES] * sin)[:, 0:QK_ROPE]
        krf_ref[t] = kr
        krb_ref[t] = kr.astype(BF16)

    zp = [pbuf_ref[k] for k in range(POOL_BUF)] + [p[:, OFF_PD:OFF_PD + GW] for p in projs]
    wins = _pool_windows()
    for t in range(steps):
        sums = []
        run = zp[POOL_BUF + t]
        for j in range(1, POOL_WINDOWS[-1]):
            if j in POOL_WINDOWS:
                sums.append(run)
            run = run + zp[POOL_BUF + t - j]
        sums.append(run)
        cnt = jnp.minimum(float(start + t + 1), wins)
        d = (_pool_select(*sums) / cnt - zp[POOL_BUF + t]).astype(BF16)
        abd_ref[t, :, 2 * GW:3 * GW] = (_dot(d, wpool_ref[...]) * vec(V_PSCALE)).astype(BF16)
    for k in range(POOL_BUF):
        npool_ref[k] = zp[steps + k]


def _premix_sample(x_tm, mod, lw, sgu_coef, sgu_bias, cos, sin, cbuf, h0, pbuf, start):
    steps, bd, d = x_tm.shape
    assert steps <= min(CHUNK, NEW_KEY_PAD)
    f = lambda w, dt: jax.ShapeDtypeStruct((steps, bd, w), dt)
    out_shape = [f(3 * GW, BF16), f(N_HEADS * GW, BF16), f(ROPE_LANES, BF16), f(GW, F32), f(GW, BF16),
                 f(QK_ROPE, F32), f(QK_ROPE, BF16), f(GW, F32),
                 jax.ShapeDtypeStruct((CONV_WIDTH - 1, bd, GW), F32),
                 jax.ShapeDtypeStruct((bd, GW), F32),
                 jax.ShapeDtypeStruct((POOL_BUF, bd, GW), F32)]
    args = (x_tm, mod, lw["w_in"], sgu_coef, sgu_bias, lw["vec"], lw["w_gate"], lw["b_gate"],
            lw["w_q"], lw["w_uk"], lw["w_pool"], cos, sin, cbuf, h0, pbuf)
    in_specs = [_full(a.shape) for a in args]
    in_specs[1] = pl.BlockSpec((bd, 2 * d), lambda i: (0, 0))
    return pl.pallas_call(
        functools.partial(_premix_sample_kernel, steps=steps, start=start),
        grid=(1,), in_specs=in_specs,
        out_specs=[_full(s.shape) for s in out_shape], out_shape=out_shape,
        compiler_params=_cparams(("arbitrary",)),
        name="premix_sample",
    )(*args)


def _decode_kernel(pt_ref, ql_ref, qr_ref, nk_ref, nkr_ref, *rest, pages, steps):
    del pt_ref
    ckv_pages = rest[:pages]
    kr_pages = rest[pages:2 * pages]
    o_ref = rest[2 * pages]
    m_ref, l_ref, acc_ref = rest[2 * pages + 1:]
    j = pl.program_id(1)
    rows = N_HEADS * steps

    @pl.when(j == 0)
    def _():
        m_ref[...] = jnp.full((2, rows, 1), -jnp.inf, F32)
        l_ref[...] = jnp.zeros((2, rows, 1), F32)
        acc_ref[...] = jnp.zeros((2, rows, GW), F32)

    ql = ql_ref[0]
    qr = qr_ref[0]

    def update(c, s, values):
        m_old = m_ref[c]
        m_new = jnp.maximum(m_old, jnp.max(s, -1, keepdims=True))
        alpha = jnp.exp(m_old - m_new)
        p = jnp.exp(s - m_new)
        l_ref[c] = alpha * l_ref[c] + jnp.sum(p, -1, keepdims=True)
        p = p.astype(BF16)
        pv = None
        lo = 0
        for v in values:
            part = _dot(p[:, lo:lo + v.shape[0]], v)
            pv = part if pv is None else pv + part
            lo += v.shape[0]
        acc_ref[c] = alpha * acc_ref[c] + pv
        m_ref[c] = m_new

    half = pages // 2
    for c in range(2):
        ks = [r[...].astype(BF16) for r in ckv_pages[c * half:(c + 1) * half]]
        krs = [r[...].astype(BF16) for r in kr_pages[c * half:(c + 1) * half]]
        s = jnp.concatenate([_dot_nt(ql, k) + _dot(qr, krt) for k, krt in zip(ks, krs)], -1)
        update(c, s, ks)

    @pl.when(j == pl.num_programs(1) - 1)
    def _():
        nk = nk_ref[0]
        s_new = _dot_nt(ql, nk) + _dot_nt(qr, nkr_ref[0])
        qstep = _mod(lax.broadcasted_iota(jnp.int32, (rows, NEW_KEY_PAD), 0), steps)
        kstep = lax.broadcasted_iota(jnp.int32, (rows, NEW_KEY_PAD), 1)
        update(0, jnp.where(kstep <= qstep, s_new, -jnp.inf), [nk])
        m = jnp.maximum(m_ref[0], m_ref[1])
        w0 = jnp.exp(m_ref[0] - m)
        w1 = jnp.exp(m_ref[1] - m)
        o_ref[0] = (w0 * acc_ref[0] + w1 * acc_ref[1]) / (w0 * l_ref[0] + w1 * l_ref[1])


def _decode_attn(layer, page_table, cache_ckv, cache_krt, qlat, qrope, nk, nkr):
    bd, rows, _ = qlat.shape
    n_pages = page_table.shape[1]
    pages = min(DECODE_PAGES, n_pages)
    assert n_pages % pages == 0 and pages % 2 == 0
    steps = rows // N_HEADS

    def page_spec(shape, k):
        return pl.BlockSpec((None, None) + shape, lambda i, j, pt: (layer, pt[i, j * pages + k], 0, 0))

    per_seq = lambda r, w: pl.BlockSpec((1, r, w), lambda i, j, pt: (i, 0, 0))
    in_specs = ([per_seq(rows, GW), per_seq(rows, QK_ROPE), per_seq(NEW_KEY_PAD, GW), per_seq(NEW_KEY_PAD, QK_ROPE)]
                + [page_spec((PAGE_SIZE, GW), k) for k in range(pages)]
                + [page_spec((QK_ROPE, PAGE_SIZE), k) for k in range(pages)])
    grid_spec = pltpu.PrefetchScalarGridSpec(
        num_scalar_prefetch=1, grid=(bd, n_pages // pages),
        in_specs=in_specs, out_specs=per_seq(rows, GW),
        scratch_shapes=[pltpu.VMEM((2, rows, 1), F32), pltpu.VMEM((2, rows, 1), F32),
                        pltpu.VMEM((2, rows, GW), F32)])
    return pl.pallas_call(
        functools.partial(_decode_kernel, pages=pages, steps=steps),
        grid_spec=grid_spec,
        out_shape=jax.ShapeDtypeStruct((bd, rows, GW), F32),
        compiler_params=_cparams(("arbitrary", "arbitrary")),
        name="decode_attn",
    )(page_table, qlat, qrope, nk, nkr, *([cache_ckv] * pages), *([cache_krt] * pages))


def _uv_kernel(o_ref, wuv_ref, out_ref):
    out = _dot(o_ref[0].astype(BF16), wuv_ref[0])
    for hh in range(1, N_HEADS):
        out = out + _dot(o_ref[hh].astype(BF16), wuv_ref[hh])
    out_ref[...] = out.astype(BF16)


def _uv_project(o_heads, wuv):
    _, n, _ = o_heads.shape
    return pl.pallas_call(
        _uv_kernel, grid=(1,),
        in_specs=[_full(o_heads.shape), _full(wuv.shape)],
        out_specs=_full((n, GW)), out_shape=jax.ShapeDtypeStruct((n, GW), BF16),
        compiler_params=_cparams(("arbitrary",)), name="uv_project",
    )(o_heads, wuv)


def _mix_norm(x_ref, abd_ref, c_ref, moda_ref, modb_ref, wabd_ref, wc_ref, lnv_ref):
    mix = _dot(abd_ref[...], wabd_ref[...]) + _dot(c_ref[...], wc_ref[...])
    g1 = moda_ref[0, :, 0:D_MODEL]
    sh2 = moda_ref[0, :, D_MODEL:2 * D_MODEL]
    sc2 = modb_ref[0, :, 0:D_MODEL]
    x1 = _ln(ALPHA * x_ref[...] + g1 * mix) * lnv_ref[0:1, :] + lnv_ref[1:2, :]
    h2 = (_ln(x1) * (1.0 + sc2) + sh2).astype(BF16)
    return x1, h2


def _swiglu(h2, w1, w3, w2):
    return _dot((_silu(_dot(h2, w1)) * _dot(h2, w3)).astype(BF16), w2)


def _post_dense_kernel(x_ref, abd_ref, c_ref, moda_ref, modb_ref, wabd_ref, wc_ref, lnv_ref,
                       w1_ref, w3_ref, w2_ref, o_ref):
    x1, h2 = _mix_norm(x_ref, abd_ref, c_ref, moda_ref, modb_ref, wabd_ref, wc_ref, lnv_ref)
    f = _swiglu(h2, w1_ref[...], w3_ref[...], w2_ref[...])
    g2 = modb_ref[0, :, D_MODEL:2 * D_MODEL]
    o_ref[...] = _ln(ALPHA * x1 + g2 * f) * lnv_ref[2:3, :] + lnv_ref[3:4, :]


def _token_specs(tm, tiles_per_group, mod_rows):
    mi = lambda blk: (lambda i, *_: (i // tiles_per_group, 0, blk))
    tok = lambda w: pl.BlockSpec((tm, w), lambda i, *_: (i, 0))
    mod = lambda blk: pl.BlockSpec((1, mod_rows, 2 * D_MODEL), mi(blk))
    return tok, mod


def _post_dense(x, abd, c, mod, lw, fw, tm, tiles_per_group):
    n, d = x.shape
    tok, modspec = _token_specs(tm, tiles_per_group, mod.shape[1])
    weights = (lw["w_abd"], lw["w_c"], lw["lnv"], fw["w1"], fw["w3"], fw["w2"])
    return pl.pallas_call(
        _post_dense_kernel, grid=(n // tm,),
        in_specs=[tok(d), tok(3 * GW), tok(GW), modspec(1), modspec(2)] + [_full(w.shape) for w in weights],
        out_specs=tok(d), out_shape=jax.ShapeDtypeStruct((n, d), F32),
        compiler_params=_cparams(("arbitrary",)), name="post_dense",
    )(x, abd, c, mod, mod, *weights)


def _post_route_kernel(x_ref, abd_ref, c_ref, moda_ref, modb_ref, wabd_ref, wc_ref, lnv_ref,
                       wr_ref, br_ref, x1_ref, h2_ref, dg_ref):
    x1, h2 = _mix_norm(x_ref, abd_ref, c_ref, moda_ref, modb_ref, wabd_ref, wc_ref, lnv_ref)
    x1_ref[...] = x1
    h2_ref[...] = h2
    lane = lax.broadcasted_iota(jnp.int32, (1, 128), 1)
    logits = jnp.where(lane < N_EXPERTS, _dot(h2, wr_ref[...]) + br_ref[...], -jnp.inf)
    m1 = jnp.max(logits, -1, keepdims=True)
    i1 = jnp.min(jnp.where(logits == m1, lane, 128), -1, keepdims=True)
    rest = jnp.where(lane == i1, -jnp.inf, logits)
    m2 = jnp.max(rest, -1, keepdims=True)
    i2 = jnp.min(jnp.where(rest == m2, lane, 128), -1, keepdims=True)
    e2 = jnp.exp(m2 - m1)
    gate1 = 1.0 / (1.0 + e2)
    gate2 = e2 / (1.0 + e2)
    dg_ref[...] = jnp.where(lane == i1, gate1, 0.0) + jnp.where(lane == i2, gate2, 0.0)


def _post_route(x, abd, c, mod, lw, mw, tm, tiles_per_group):
    n, d = x.shape
    tok, modspec = _token_specs(tm, tiles_per_group, mod.shape[1])
    weights = (lw["w_abd"], lw["w_c"], lw["lnv"], mw["w_router"], mw["b_router"])
    return pl.pallas_call(
        _post_route_kernel, grid=(n // tm,),
        in_specs=[tok(d), tok(3 * GW), tok(GW), modspec(1), modspec(2)] + [_full(w.shape) for w in weights],
        out_specs=[tok(d), tok(d), tok(128)],
        out_shape=[jax.ShapeDtypeStruct((n, d), F32), jax.ShapeDtypeStruct((n, d), BF16),
                   jax.ShapeDtypeStruct((n, 128), F32)],
        compiler_params=_cparams(("arbitrary",)), name="post_route",
    )(x, abd, c, mod, mod, *weights)


def _moe_dense_kernel(x1_ref, h2_ref, dg_ref, modb_ref, lnv_ref, w1_ref, w3_ref, w2_ref, o_ref, acc_ref):
    e = pl.program_id(1)

    @pl.when(e == 0)
    def _():
        acc_ref[...] = jnp.zeros_like(acc_ref)

    lane = lax.broadcasted_iota(jnp.int32, (1, 128), 1)
    gate = jnp.sum(jnp.where(lane == e, dg_ref[...], 0.0), -1, keepdims=True)
    acc_ref[...] += gate * _swiglu(h2_ref[...], w1_ref[...], w3_ref[...], w2_ref[...])

    @pl.when(e == pl.num_programs(1) - 1)
    def _():
        g2 = modb_ref[0, :, D_MODEL:2 * D_MODEL]
        o_ref[...] = _ln(ALPHA * x1_ref[...] + g2 * acc_ref[...]) * lnv_ref[2:3, :] + lnv_ref[3:4, :]


def _moe_dense(x1, h2, dg, mod, lw, mw, tm, tiles_per_group):
    n, d = x1.shape
    tok, modspec = _token_specs(tm, tiles_per_group, mod.shape[1])
    wspec = lambda shape: pl.BlockSpec((None,) + shape, lambda i, e: (e, 0, 0))
    return pl.pallas_call(
        _moe_dense_kernel, grid=(n // tm, N_EXPERTS),
        in_specs=[tok(d), tok(d), tok(128), modspec(2), _full(lw["lnv"].shape),
                  wspec((d, D_FF)), wspec((d, D_FF)), wspec((D_FF, d))],
        out_specs=tok(d), out_shape=jax.ShapeDtypeStruct((n, d), F32),
        scratch_shapes=[pltpu.VMEM((tm, d), F32)],
        compiler_params=_cparams(("arbitrary", "arbitrary")), name="moe_dense",
    )(x1, h2, dg, mod, lw["lnv"], mw["w1"], mw["w3"], mw["w2"])


def _block_diag(blocks):
    n, r, c = blocks.shape
    out = jnp.zeros((n * r, n * c), blocks.dtype)
    for i in range(n):
        out = out.at[i * r:(i + 1) * r, i * c:(i + 1) * c].set(blocks[i])
    return out


def _swap_rope_halves(w):
    half = QK_ROPE // 2
    return jnp.concatenate([w[..., half:], w[..., :half]], -1)


def _layer_weights(l, p):
    w_in = p["w_in"][l]
    s = np.cumsum([GW, GW, GW, GW, GW, GW, QK_ROPE])
    kr_cols = w_in[:, s[5]:s[6]]
    gap = jnp.zeros_like(kr_cols)
    kr_sw = _swap_rope_halves(kr_cols)
    w_in_r = jnp.concatenate(
        [w_in[:, :s[5]], w_in[:, s[6]:], kr_cols, gap, kr_cols, gap, kr_sw, gap, kr_sw, gap], 1).astype(BF16)
    vec = jnp.zeros((VEC_ROWS, GW), F32)
    for i, name in ((V_SGU_G, "sgu_ln_g"), (V_SGU_B, "sgu_ln_b"), (V_CONV_B, "lru_conv_b"), (V_LAM, "lru_lambda"),
                    (V_QNORM, "mla_q_norm"), (V_KVNORM, "mla_kv_norm"), (V_PSCALE, "pool_scale")):
        vec = vec.at[i].set(p[name][l])
    vec = vec.at[V_CONV_W:V_CONV_W + CONV_WIDTH].set(p["lru_conv_w"][l])
    w_uq = p["mla_w_uq"][l]
    q_rope_cols = w_uq[:, :, QK_NOPE:]
    w_q = jnp.concatenate([w_uq[:, :, :QK_NOPE].reshape(GW, GW), q_rope_cols.reshape(GW, ROPE_LANES),
                           _swap_rope_halves(q_rope_cols).reshape(GW, ROPE_LANES)], 1).astype(BF16)
    w_uk = _block_diag(jnp.transpose(p["mla_w_uk"][l], (1, 2, 0))).astype(BF16)
    w_uv = p["mla_w_uv"][l]
    zq = jnp.zeros((GW, N_HEADS, HEAD_LANES - QK_NOPE - QK_ROPE), F32)
    w_qh = jnp.concatenate(
        [jnp.concatenate([w_uq, zq], -1).reshape(GW, N_HEADS * HEAD_LANES),
         jnp.concatenate([jnp.zeros((GW, N_HEADS, QK_NOPE), F32), _swap_rope_halves(q_rope_cols), zq],
                         -1).reshape(GW, N_HEADS * HEAD_LANES)], 1).astype(BF16)
    w_kn = jnp.concatenate([p["mla_w_uk"][l], jnp.zeros((GW, N_HEADS, HEAD_LANES - QK_NOPE), F32)], -1)
    w_kv = jnp.concatenate([w_kn.reshape(GW, N_HEADS * HEAD_LANES), w_uv.reshape(GW, GW)], 1).astype(BF16)
    w_uv_pad = jnp.zeros((N_HEADS, GW, GW), F32)
    for hh in range(N_HEADS):
        w_uv_pad = w_uv_pad.at[hh, :, hh * HEAD_DIM:(hh + 1) * HEAD_DIM].set(w_uv[:, hh, :])
    w_out = p["w_out"][l]
    return {
        "w_in": w_in_r,
        "sgu_w": p["sgu_w"][l],
        "sgu_b": p["sgu_b"][l],
        "vec": vec,
        "w_gate": jnp.concatenate([_block_diag(p["lru_wa"][l]), _block_diag(p["lru_wx"][l])], 1).astype(BF16),
        "b_gate": jnp.concatenate([p["lru_ba"][l], p["lru_bx"][l]])[None, :],
        "w_q": w_q,
        "w_uk": w_uk,
        "w_qh": w_qh,
        "w_kv": w_kv,
        "w_uv": w_uv_pad.astype(BF16),
        "w_pool": _block_diag(p["pool_w"][l]).astype(BF16),
        "w_abd": jnp.concatenate([w_out[:2 * GW], w_out[3 * GW:]], 0).astype(BF16),
        "w_c": w_out[2 * GW:3 * GW].astype(BF16),
        "lnv": jnp.stack([p["ln1_g"][l], p["ln1_b"][l], p["ln2_g"][l], p["ln2_b"][l]]),
    }


def _rope_tables(start, steps):
    half = QK_ROPE // 2
    freq = ROPE_THETA ** (-jnp.arange(half, dtype=F32) / half)
    ang = (start + jnp.arange(steps)).astype(F32)[:, None] * freq[None, :]
    cos = jnp.cos(ang)
    sin = jnp.sin(ang)
    return jnp.concatenate([cos, cos], -1), jnp.concatenate([-sin, sin], -1)


def _rope_tiled(start, steps):
    cos, sin = _rope_tables(start, steps)
    return jnp.tile(cos, (1, N_HEADS)), jnp.tile(sin, (1, N_HEADS))


def _rope_head_tables(start, steps):
    cos, sin = _rope_tables(start, steps)
    one = jnp.ones((steps, QK_NOPE), F32)
    z64 = jnp.zeros((steps, QK_NOPE), F32)
    z32 = jnp.zeros((steps, QK_ROPE), F32)
    return jnp.concatenate([one, cos, z32, z64, sin, z32, cos, z32, cos, z32, sin, z32, sin, z32], 1)


def kernel(x_prompt, x_sample, c_prompt, c_sample, cache_ckv, cache_kr, page_table, state_lru_conv, state_lru_h, state_pool, ada_w, ada_b, w_in, w_out, ln1_g, ln1_b, ln2_g, ln2_b, sgu_ln_g, sgu_ln_b, sgu_w, sgu_b, lru_conv_w, lru_conv_b, lru_wa, lru_ba, lru_wx, lru_bx, lru_lambda, mla_q_norm, mla_w_uq, mla_kv_norm, mla_w_uk, mla_w_uv, pool_w, pool_scale, ffn_w1, ffn_w3, ffn_w2, moe_router, moe_router_b, moe_w1, moe_w3, moe_w2):
    p = dict(w_in=w_in, w_out=w_out, ln1_g=ln1_g, ln1_b=ln1_b, ln2_g=ln2_g, ln2_b=ln2_b,
             sgu_ln_g=sgu_ln_g, sgu_ln_b=sgu_ln_b, sgu_w=sgu_w, sgu_b=sgu_b, lru_conv_w=lru_conv_w,
             lru_conv_b=lru_conv_b, lru_wa=lru_wa, lru_ba=lru_ba, lru_wx=lru_wx, lru_bx=lru_bx,
             lru_lambda=lru_lambda, mla_q_norm=mla_q_norm, mla_w_uq=mla_w_uq, mla_kv_norm=mla_kv_norm,
             mla_w_uk=mla_w_uk, mla_w_uv=mla_w_uv, pool_w=pool_w, pool_scale=pool_scale)
    bp, tp, d = x_prompt.shape
    bd, td, _ = x_sample.shape
    n_pages = page_table.shape[1]
    past_len = n_pages * PAGE_SIZE
    tmp = PROMPT_TILE

    rope_p = _rope_head_tables(0, tp)
    cos_s, sin_s = _rope_tiled(past_len, td)
    cache_krt = jnp.swapaxes(cache_kr, 2, 3)
    pad = (-(bd + bp)) % 8
    c_all = jnp.concatenate([c_sample, c_prompt, jnp.zeros((pad, d), F32)], 0)

    xp = x_prompt
    xs = jnp.transpose(x_sample, (1, 0, 2))
    sp = [[] for _ in range(5)]
    ss = [[] for _ in range(6)]
    for l in range(DEPTH):
        lw = _layer_weights(l, p)
        mod = _ada(c_all, ada_w[l].astype(BF16), ada_b[l][None, :])
        mod_s = mod[:bd]
        mod_p = mod[bd:bd + bp][:, None, :]

        (abd_p, qcat_p, kcat_p, vt_p, ckv_p, kr_p, nconv_p, nh_p, npool_p) = _premix_prompt(
            xp, mod_p, dict(lw, sgu_w=lw["sgu_w"][:, :CHUNK, :CHUNK],
                            sgu_bias=jnp.repeat(lw["sgu_b"][:, :CHUNK].T, HEAD_DIM, axis=1)), rope_p)
        c_p = _attn_prompt(qcat_p, kcat_p, vt_p)

        sgu_coef = jnp.repeat(jnp.transpose(lw["sgu_w"][:, :td, :td], (1, 2, 0)), HEAD_DIM, axis=2)
        sgu_bias_s = jnp.repeat(lw["sgu_b"][:, :td].T, HEAD_DIM, axis=1)
        (abd_s, qlat_s, qrope_s, ckv_s, ckvb_s, kr_s, krb_s, vrows_s, nconv_s, nh_s, npool_s) = _premix_sample(
            xs, mod_s, lw, sgu_coef, sgu_bias_s, cos_s, sin_s,
            jnp.transpose(state_lru_conv[l], (1, 0, 2)), state_lru_h[l],
            jnp.transpose(state_pool[l], (1, 0, 2)), past_len)
        to_rows = lambda a, w: jnp.transpose(a.reshape(td, bd, N_HEADS, w), (1, 2, 0, 3)).reshape(bd, N_HEADS * td, w)
        pad_keys = lambda a: jnp.pad(jnp.transpose(a, (1, 0, 2)), ((0, 0), (0, NEW_KEY_PAD - td), (0, 0)))
        o_s = _decode_attn(l, page_table, cache_ckv, cache_krt, to_rows(qlat_s, GW), to_rows(qrope_s, QK_ROPE),
                           pad_keys(ckvb_s), pad_keys(krb_s))
        o_heads = jnp.transpose(o_s.reshape(bd, N_HEADS, td, GW), (1, 2, 0, 3)).reshape(N_HEADS, td * bd, GW)
        c_s = _uv_project(o_heads, lw["w_uv"])

        xp2 = xp.reshape(bp * tp, d)
        xs2 = xs.reshape(td * bd, d)
        args_p = (xp2, abd_p.reshape(bp * tp, 3 * GW), c_p.reshape(bp * tp, GW), mod_p)
        args_s = (xs2, abd_s.reshape(td * bd, 3 * GW), c_s, mod_s[None])
        j = l // 2
        if l % 2 == 0:
            fw = {"w1": ffn_w1[j].astype(BF16), "w3": ffn_w3[j].astype(BF16), "w2": ffn_w2[j].astype(BF16)}
            xp2 = _post_dense(*args_p, lw, fw, tmp, tp // tmp)
            xs2 = _post_dense(*args_s, lw, fw, bd, td)
        else:
            mw = {"w_router": jnp.pad(moe_router[j], ((0, 0), (0, 128 - N_EXPERTS))).astype(BF16),
                  "b_router": jnp.pad(moe_router_b[j], (0, 128 - N_EXPERTS))[None, :],
                  "w1": moe_w1[j].astype(BF16), "w3": moe_w3[j].astype(BF16), "w2": moe_w2[j].astype(BF16)}
            x1p, h2p, dgp = _post_route(*args_p, lw, mw, tmp, tp // tmp)
            x1s, h2s, dgs = _post_route(*args_s, lw, mw, bd, td)
            xp2 = _moe_dense(x1p, h2p, dgp, mod_p, lw, mw, tmp, tp // tmp)
            xs2 = _moe_dense(x1s, h2s, dgs, mod_s[None], lw, mw, bd, td)
        xp = xp2.reshape(bp, tp, d)
        xs = xs2.reshape(td, bd, d)

        for k, v in enumerate((ckv_p, kr_p, nconv_p, nh_p[:, 0], npool_p)):
            sp[k].append(v)
        bt = lambda a: jnp.transpose(a, (1, 0, 2))
        for k, v in enumerate((bt(ckv_s), bt(kr_s), bt(nconv_s), nh_s, bt(npool_s), bt(vrows_s))):
            ss[k].append(v)

    return (xp, jnp.transpose(xs, (1, 0, 2)),
            jnp.stack(sp[0]), jnp.stack(sp[1]), jnp.stack(sp[2]), jnp.stack(sp[3]), jnp.stack(sp[4]),
            jnp.stack(ss[0]), jnp.stack(ss[1]), jnp.stack(ss[2]), jnp.stack(ss[3]), jnp.stack(ss[4]),
            jnp.stack(ss[5]))
```

```python
import functools

import jax
import jax.numpy as jnp
import numpy as np
from jax import lax
from jax.experimental import pallas as pl
from jax.experimental.pallas import tpu as pltpu

F32 = jnp.float32
BF16 = jnp.bfloat16

D_MODEL = 1024
DEPTH = 2
PAGE_SIZE = 128
GROUP_WIDTH = D_MODEL // 4
HEAD_DIM = 64
N_HEADS = GROUP_WIDTH // HEAD_DIM
CHUNK = 128
CONV_WIDTH = 4
LRU_C = 8.0
QK_NOPE = HEAD_DIM
QK_ROPE = HEAD_DIM // 2
MLA_SCALE = (QK_NOPE + QK_ROPE) ** -0.5
ROPE_THETA = 10000.0
POOL_WINDOWS = (2, 4, 8, 16)
POOL_BUF = 15
POOL_GC = GROUP_WIDTH // 4
D_FF = ((8 * D_MODEL // 3 + 127) // 128) * 128
N_EXPERTS = 8
ALPHA = (2 * DEPTH) ** 0.25
LN_EPS = 1e-5
RMS_EPS = 1e-6

GW = GROUP_WIDTH
OFF_U, OFF_V, OFF_XB, OFF_GB, OFF_CQ, OFF_CKV, OFF_PD, OFF_KRA, OFF_KRB = (
    0, GW, 2 * GW, 3 * GW, 4 * GW, 5 * GW, 6 * GW, 7 * GW, 7 * GW + 128)
P_IN_PAD = 8 * GW
ROPE_LANES = N_HEADS * QK_ROPE
HEAD_LANES = 128

V_SGU_G, V_SGU_B, V_CONV_B, V_LAM, V_QNORM, V_KVNORM, V_PSCALE = 0, 1, 2, 3, 4, 5, 6
V_CONV_W = 8
VEC_ROWS = 16

VMEM_LIMIT_BYTES = 56 * 1024 * 1024
PROMPT_TILE = 256
ATTN_TILE = 256
DECODE_PAGES = 16
NEW_KEY_PAD = 16
MASK_FLOOR = -3.0e38
MOE_TILE = 256
MOE_BLOCK = 512


def _dot(a, b):
    return jnp.dot(a, b, preferred_element_type=F32)


def _dot_nt(a, b):
    return lax.dot_general(a, b, (((1,), (1,)), ((), ())), preferred_element_type=F32)


def _ln(x):
    mu = jnp.mean(x, -1, keepdims=True)
    xc = x - mu
    var = jnp.mean(xc * xc, -1, keepdims=True)
    return xc * lax.rsqrt(var + LN_EPS)


def _rms(x):
    return x * lax.rsqrt(jnp.mean(x * x, -1, keepdims=True) + RMS_EPS)


def _gelu(x):
    return 0.5 * x * (1.0 + jnp.tanh(0.7978845608028654 * (x + 0.044715 * (x * x * x))))


def _silu(x):
    return x * jax.nn.sigmoid(x)


def _softplus(x):
    return jnp.maximum(x, 0.0) + jnp.log1p(jnp.exp(-jnp.abs(x)))


def _lane_group(width, group):
    assert group & (group - 1) == 0
    return lax.broadcasted_iota(jnp.int32, (1, width), 1) >> (group.bit_length() - 1)


def _mod(x, n):
    return x & (n - 1) if n & (n - 1) == 0 else lax.rem(x, n)


def _cparams(sem):
    return pltpu.CompilerParams(dimension_semantics=sem, vmem_limit_bytes=VMEM_LIMIT_BYTES)


def _full(shape):
    n = len(shape)
    return pl.BlockSpec(shape, lambda *_: (0,) * n)


def _ada_kernel(c_ref, w_ref, b_ref, o_ref):
    o_ref[...] = _dot(_silu(c_ref[...]).astype(BF16), w_ref[...]) + b_ref[...]


def _ada(c_all, w, b):
    m, d = c_all.shape
    n = w.shape[1]
    tn = 512
    return pl.pallas_call(
        _ada_kernel,
        grid=(n // tn,),
        in_specs=[pl.BlockSpec((m, d), lambda j: (0, 0)),
                  pl.BlockSpec((d, tn), lambda j: (0, j)),
                  pl.BlockSpec((1, tn), lambda j: (0, j))],
        out_specs=pl.BlockSpec((m, tn), lambda j: (0, j)),
        out_shape=jax.ShapeDtypeStruct((m, n), F32),
        compiler_params=_cparams(("arbitrary",)),
        name="ada",
    )(c_all, w, b)


def _lru_gates(xc, wgate, bgate, lam):
    g = _dot(xc.astype(BF16), wgate) + bgate
    r = jax.nn.sigmoid(g[:, :GW])
    i = jax.nn.sigmoid(g[:, GW:])
    log_a = (-LRU_C) * r * _softplus(-lam)
    a = jnp.exp(log_a)
    bt = jnp.sqrt(-jnp.tanh(log_a) * (a * a + 1.0)) * i * xc
    return a, bt


def _mla_queries(cq, qnorm, wq, wuk, cos, sin):
    q = _dot((_rms(cq) * qnorm).astype(BF16), wq)
    qlat = _dot(q[:, :GW].astype(BF16), wuk) * MLA_SCALE
    qr = (q[:, GW:GW + ROPE_LANES] * cos + q[:, GW + ROPE_LANES:] * sin) * MLA_SCALE
    return qlat, qr


def _pool_select(s2, s4, s8, s16):
    grp = _lane_group(GW, POOL_GC)
    return jnp.where(grp == 0, s2, jnp.where(grp == 1, s4, jnp.where(grp == 2, s8, s16)))


def _pool_windows():
    grp = _lane_group(GW, POOL_GC)
    w = jnp.where(grp == 0, POOL_WINDOWS[0],
                  jnp.where(grp == 1, POOL_WINDOWS[1],
                            jnp.where(grp == 2, POOL_WINDOWS[2], POOL_WINDOWS[3])))
    return w.astype(F32)


def _premix_prompt_kernel(x_ref, mod_ref, win_ref, sguw_ref, sgub_ref, vec_ref, wgate_ref, bgate_ref,
                          wq_ref, wkv_ref, wpool_ref, rope_ref,
                          abd_ref, qcat_ref, kcat_ref, vt_ref, ckvf_ref, krf_ref,
                          nconv_ref, nh_ref, npool_ref,
                          zc_ref, zp_ref, hc_ref, *, tm):
    t = pl.program_id(1)

    @pl.when(t == 0)
    def _():
        zc_ref[0:8, :] = jnp.zeros((8, GW), F32)
        zp_ref[0:16, :] = jnp.zeros((16, GW), F32)
        hc_ref[...] = jnp.zeros((1, GW), F32)

    def vec(i):
        return vec_ref[i:i + 1, :]

    x = x_ref[0]
    sh1 = mod_ref[0, :, 0:D_MODEL]
    sc1 = mod_ref[0, :, D_MODEL:2 * D_MODEL]
    h = _ln(x) * (1.0 + sc1) + sh1
    proj = _dot(h.astype(BF16), win_ref[...])

    ug = _gelu(proj[:, OFF_U:OFF_U + GW])
    vg = _ln(_gelu(proj[:, OFF_V:OFF_V + GW])) * vec(V_SGU_G) + vec(V_SGU_B)
    vgb = vg.astype(BF16)
    head = _lane_group(GW, HEAD_DIM)
    row = lax.broadcasted_iota(jnp.int32, (CHUNK, CHUNK), 0)
    col = lax.broadcasted_iota(jnp.int32, (CHUNK, CHUNK), 1)
    wmix = [jnp.where(col <= row, sguw_ref[hh], 0.0).astype(BF16) for hh in range(N_HEADS)]
    parts = []
    for c in range(tm // CHUNK):
        vc = vgb[c * CHUNK:(c + 1) * CHUNK]
        acc = _dot(wmix[0], vc)
        for hh in range(1, N_HEADS):
            acc = jnp.where(head == hh, _dot(wmix[hh], vc), acc)
        parts.append(acc + sgub_ref[...])
    mixed = jnp.concatenate(parts, 0) if len(parts) > 1 else parts[0]
    abd_ref[0, :, 0:GW] = (ug * mixed).astype(BF16)

    xb = proj[:, OFF_XB:OFF_XB + GW]
    zc_ref[8:8 + tm, :] = xb
    z = zc_ref[...]
    xc = (vec(V_CONV_B) + vec(V_CONV_W + 3) * z + vec(V_CONV_W + 2) * pltpu.roll(z, 1, 0)
          + vec(V_CONV_W + 1) * pltpu.roll(z, 2, 0) + vec(V_CONV_W) * pltpu.roll(z, 3, 0))[8:]
    nconv_ref[0] = zc_ref[pl.ds(8 + tm - (CONV_WIDTH - 1), CONV_WIDTH - 1), :]
    zc_ref[0:8, :] = xb[tm - 8:tm]
    a, bt = _lru_gates(xc, wgate_ref[...], bgate_ref[...], vec(V_LAM))
    rowi = lax.broadcasted_iota(jnp.int32, (tm, GW), 0)
    k = 1
    while k < tm:
        keep = rowi >= k
        bt = jnp.where(keep, a * pltpu.roll(bt, k, 0) + bt, bt)
        a = jnp.where(keep, a * pltpu.roll(a, k, 0), a)
        k *= 2
    hs = a * hc_ref[...] + bt
    hc_ref[...] = hs[tm - 1:tm]
    nh_ref[0] = hs[tm - 1:tm]
    abd_ref[0, :, GW:2 * GW] = (hs * _gelu(proj[:, OFF_GB:OFF_GB + GW])).astype(BF16)

    cosq = rope_ref[:, 0:HEAD_LANES]
    sinq = rope_ref[:, HEAD_LANES:2 * HEAD_LANES]
    cosk = rope_ref[:, 2 * HEAD_LANES:3 * HEAD_LANES]
    sink = rope_ref[:, 3 * HEAD_LANES:4 * HEAD_LANES]
    q = _dot((_rms(proj[:, OFF_CQ:OFF_CQ + GW]) * vec(V_QNORM)).astype(BF16), wq_ref[...])
    half = N_HEADS * HEAD_LANES
    for hh in range(N_HEADS):
        lo = hh * HEAD_LANES
        qh = q[:, lo:lo + HEAD_LANES] * cosq + q[:, half + lo:half + lo + HEAD_LANES] * sinq
        qcat_ref[0, hh] = (qh * MLA_SCALE).astype(BF16)
    ckv = _rms(proj[:, OFF_CKV:OFF_CKV + GW]) * vec(V_KVNORM)
    ckvf_ref[0] = ckv
    kv = _dot(ckv.astype(BF16), wkv_ref[...])
    kr = proj[:, OFF_KRA:OFF_KRA + HEAD_LANES] * cosk + proj[:, OFF_KRB:OFF_KRB + HEAD_LANES] * sink
    krf_ref[0] = kr[:, 0:QK_ROPE]
    kr_hi = jnp.where(lax.broadcasted_iota(jnp.int32, (1, HEAD_LANES), 1) >= QK_NOPE, kr, 0.0)
    for hh in range(N_HEADS):
        kcat_ref[0, hh] = (kv[:, hh * HEAD_LANES:(hh + 1) * HEAD_LANES] + kr_hi).astype(BF16)
    vt_ref[0, 0] = kv[:, half:half + GW].T.astype(BF16)

    pd = proj[:, OFF_PD:OFF_PD + GW]
    zp_ref[16:16 + tm, :] = pd
    zz = zp_ref[...]
    s2 = zz + pltpu.roll(zz, 1, 0)
    s4 = s2 + pltpu.roll(s2, 2, 0)
    s8 = s4 + pltpu.roll(s4, 4, 0)
    s16 = s8 + pltpu.roll(s8, 8, 0)
    wsum = _pool_select(s2, s4, s8, s16)[16:]
    pos = (t * tm + lax.broadcasted_iota(jnp.int32, (tm, 1), 0) + 1).astype(F32)
    cnt = jnp.minimum(pos, _pool_windows())
    d = (wsum / cnt - pd).astype(BF16)
    abd_ref[0, :, 2 * GW:3 * GW] = (_dot(d, wpool_ref[...]) * vec(V_PSCALE)).astype(BF16)
    npool_ref[0] = zp_ref[pl.ds(16 + tm - POOL_BUF, POOL_BUF), :]
    zp_ref[0:16, :] = pd[tm - 16:tm]


def _premix_prompt(x, mod, lw, rope):
    b, t, d = x.shape
    tm = PROMPT_TILE
    assert t % tm == 0 and tm % CHUNK == 0 and tm == ATTN_TILE
    grid = (b, t // tm)
    tok = lambda w: pl.BlockSpec((1, tm, w), lambda i, j: (i, j, 0))
    per_b = lambda r: pl.BlockSpec((1, r, GW), lambda i, j: (i, 0, 0))
    heads = pl.BlockSpec((1, N_HEADS, tm, HEAD_LANES), lambda i, j: (i, 0, j, 0))
    in_specs = [
        tok(d),
        pl.BlockSpec((1, 1, 2 * d), lambda i, j: (i, 0, 0)),
        _full(lw["w_in"].shape), _full(lw["sgu_w"].shape), _full(lw["sgu_bias"].shape),
        _full(lw["vec"].shape), _full(lw["w_gate"].shape), _full(lw["b_gate"].shape),
        _full(lw["w_qh"].shape), _full(lw["w_kv"].shape), _full(lw["w_pool"].shape),
        pl.BlockSpec((tm, 4 * HEAD_LANES), lambda i, j: (j, 0)),
    ]
    out_specs = [
        tok(3 * GW), heads, heads,
        pl.BlockSpec((1, 1, GW, tm), lambda i, j: (i, j, 0, 0)),
        tok(GW), tok(QK_ROPE),
        per_b(CONV_WIDTH - 1), per_b(1), per_b(POOL_BUF),
    ]
    out_shape = [
        jax.ShapeDtypeStruct((b, t, 3 * GW), BF16),
        jax.ShapeDtypeStruct((b, N_HEADS, t, HEAD_LANES), BF16),
        jax.ShapeDtypeStruct((b, N_HEADS, t, HEAD_LANES), BF16),
        jax.ShapeDtypeStruct((b, t // tm, GW, tm), BF16),
        jax.ShapeDtypeStruct((b, t, GW), F32),
        jax.ShapeDtypeStruct((b, t, QK_ROPE), F32),
        jax.ShapeDtypeStruct((b, CONV_WIDTH - 1, GW), F32),
        jax.ShapeDtypeStruct((b, 1, GW), F32),
        jax.ShapeDtypeStruct((b, POOL_BUF, GW), F32),
    ]
    return pl.pallas_call(
        functools.partial(_premix_prompt_kernel, tm=tm),
        grid=grid, in_specs=in_specs, out_specs=out_specs, out_shape=out_shape,
        scratch_shapes=[pltpu.VMEM((tm + 8, GW), F32), pltpu.VMEM((tm + 16, GW), F32),
                        pltpu.VMEM((1, GW), F32)],
        compiler_params=_cparams(("arbitrary", "arbitrary")),
        name="premix_prompt",
    )(x, mod, lw["w_in"], lw["sgu_w"], lw["sgu_bias"], lw["vec"], lw["w_gate"], lw["b_gate"],
      lw["w_qh"], lw["w_kv"], lw["w_pool"], rope)


def _attn_prompt_kernel(q_ref, k_ref, vt_ref, o_ref, m_ref, l_ref, acc_ref, s_ref, *, tq):
    qi = pl.program_id(1)
    cols = N_HEADS * tq
    m_ref[...] = jnp.full((1, cols), -jnp.inf, F32)
    l_ref[...] = jnp.zeros((1, cols), F32)
    acc_ref[...] = jnp.zeros((GW, tq), F32)
    qs = [q_ref[0, hh] for hh in range(N_HEADS)]

    def scores(j):
        start = pl.multiple_of(j * tq, tq)
        return jnp.concatenate(
            [_dot_nt(k_ref[0, hh, pl.ds(start, tq), :], qs[hh]) for hh in range(N_HEADS)], 1)

    def accumulate(j, st):
        m_old = m_ref[...]
        m_new = jnp.maximum(m_old, jnp.max(st, 0, keepdims=True))
        alpha = jnp.exp(m_old - m_new)
        p = jnp.exp(st - m_new)
        l_ref[...] = alpha * l_ref[...] + jnp.sum(p, 0, keepdims=True)
        m_ref[...] = m_new
        pb = p.astype(BF16)
        vt = vt_ref[0, j]
        for hh in range(N_HEADS):
            r = slice(hh * HEAD_DIM, (hh + 1) * HEAD_DIM)
            c = slice(hh * tq, (hh + 1) * tq)
            acc_ref[r, :] = alpha[:, c] * acc_ref[r, :] + _dot(vt[r, :], pb[:, c])

    s_ref[...] = scores(0)

    def body(j, carry):
        st = s_ref[...]
        s_next = scores(j + 1)
        accumulate(j, st)
        s_ref[...] = s_next
        return carry

    lax.fori_loop(0, qi, body, 0)
    kpos = lax.broadcasted_iota(jnp.int32, (tq, cols), 0)
    qpos = _mod(lax.broadcasted_iota(jnp.int32, (tq, cols), 1), tq)
    accumulate(qi, jnp.where(kpos <= qpos, s_ref[...], -jnp.inf))
    linv = 1.0 / l_ref[...]
    out = jnp.concatenate(
        [acc_ref[hh * HEAD_DIM:(hh + 1) * HEAD_DIM, :] * linv[:, hh * tq:(hh + 1) * tq] for hh in range(N_HEADS)], 0)
    o_ref[0] = out.T.astype(BF16)


def _attn_prompt(qcat, kcat, vt):
    b, _, t, _ = qcat.shape
    tq = ATTN_TILE
    assert t % tq == 0
    return pl.pallas_call(
        functools.partial(_attn_prompt_kernel, tq=tq),
        grid=(b, t // tq),
        in_specs=[pl.BlockSpec((1, N_HEADS, tq, HEAD_LANES), lambda i, j: (i, 0, j, 0)),
                  pl.BlockSpec((1, N_HEADS, t, HEAD_LANES), lambda i, j: (i, 0, 0, 0)),
                  pl.BlockSpec((1, t // tq, GW, tq), lambda i, j: (i, 0, 0, 0))],
        out_specs=pl.BlockSpec((1, tq, GW), lambda i, j: (i, j, 0)),
        out_shape=jax.ShapeDtypeStruct((b, t, GW), BF16),
        scratch_shapes=[pltpu.VMEM((1, N_HEADS * tq), F32), pltpu.VMEM((1, N_HEADS * tq), F32),
                        pltpu.VMEM((GW, tq), F32), pltpu.VMEM((tq, N_HEADS * tq), F32)],
        compiler_params=_cparams(("arbitrary", "arbitrary")),
        name="attn_prompt",
    )(qcat, kcat, vt)


def _premix_sample_kernel(x_ref, mod_ref, win_ref, sguc_ref, sgub_ref, vec_ref, wgate_ref, bgate_ref,
                          wq_ref, wuk_ref, wpool_ref, cos_ref, sin_ref, cbuf_ref, h0_ref, pbuf_ref,
                          abd_ref, qlat_ref, qrope_ref, ckvf_ref, ckvb_ref, krf_ref, krb_ref, vrows_ref,
                          nconv_ref, nh_ref, npool_ref, *, steps, start):
    def vec(i):
        return vec_ref[i:i + 1, :]

    sh1 = mod_ref[:, 0:D_MODEL]
    sc1 = mod_ref[:, D_MODEL:2 * D_MODEL]
    win = win_ref[...]
    projs = [_dot((_ln(x_ref[t]) * (1.0 + sc1) + sh1).astype(BF16), win) for t in range(steps)]

    vgs = []
    for t in range(steps):
        vg = _ln(_gelu(projs[t][:, OFF_V:OFF_V + GW])) * vec(V_SGU_G) + vec(V_SGU_B)
        vrows_ref[t] = vg
        vgs.append(vg)
    for t in range(steps):
        mixed = sgub_ref[t:t + 1, :]
        for s in range(t + 1):
            mixed = mixed + sguc_ref[t, s:s + 1, :] * vgs[s]
        abd_ref[t, :, 0:GW] = (_gelu(projs[t][:, OFF_U:OFF_U + GW]) * mixed).astype(BF16)

    z = [cbuf_ref[k] for k in range(CONV_WIDTH - 1)] + [p[:, OFF_XB:OFF_XB + GW] for p in projs]
    hstate = h0_ref[...]
    for t in range(steps):
        xc = vec(V_CONV_B) + z[t] * vec(V_CONV_W)
        for k in range(1, CONV_WIDTH):
            xc = xc + z[t + k] * vec(V_CONV_W + k)
        a, bt = _lru_gates(xc, wgate_ref[...], bgate_ref[...], vec(V_LAM))
        hstate = a * hstate + bt
        abd_ref[t, :, GW:2 * GW] = (hstate * _gelu(projs[t][:, OFF_GB:OFF_GB + GW])).astype(BF16)
    nh_ref[...] = hstate
    for k in range(CONV_WIDTH - 1):
        nconv_ref[k] = z[steps + k]

    for t in range(steps):
        cos = cos_ref[t:t + 1, :]
        sin = sin_ref[t:t + 1, :]
        p = projs[t]
        qlat, qr = _mla_queries(p[:, OFF_CQ:OFF_CQ + GW], vec(V_QNORM), wq_ref[...], wuk_ref[...], cos, sin)
        qlat_ref[t] = qlat.astype(BF16)
        qrope_ref[t] = qr.astype(BF16)
        ckv = _rms(p[:, OFF_CKV:OFF_CKV + GW]) * vec(V_KVNORM)
        ckvf_ref[t] = ckv
        ckvb_ref[t] = ckv.astype(BF16)
        kr = (p[:, OFF_KRA:OFF_KRA + ROPE_LANES] * cos + p[:, OFF_KRB:OFF_KRB + ROPE_LANES] * sin)[:, 0:QK_ROPE]
        krf_ref[t] = kr
        krb_ref[t] = kr.astype(BF16)

    zp = [pbuf_ref[k] for k in range(POOL_BUF)] + [p[:, OFF_PD:OFF_PD + GW] for p in projs]
    wins = _pool_windows()
    for t in range(steps):
        sums = []
        run = zp[POOL_BUF + t]
        for j in range(1, POOL_WINDOWS[-1]):
            if j in POOL_WINDOWS:
                sums.append(run)
            run = run + zp[POOL_BUF + t - j]
        sums.append(run)
        cnt = jnp.minimum(float(start + t + 1), wins)
        d = (_pool_select(*sums) / cnt - zp[POOL_BUF + t]).astype(BF16)
        abd_ref[t, :, 2 * GW:3 * GW] = (_dot(d, wpool_ref[...]) * vec(V_PSCALE)).astype(BF16)
    for k in range(POOL_BUF):
        npool_ref[k] = zp[steps + k]


def _premix_sample(x_tm, mod, lw, sgu_coef, sgu_bias, cos, sin, cbuf, h0, pbuf, start):
    steps, bd, d = x_tm.shape
    assert steps <= min(CHUNK, NEW_KEY_PAD)
    f = lambda w, dt: jax.ShapeDtypeStruct((steps, bd, w), dt)
    out_shape = [f(3 * GW, BF16), f(N_HEADS * GW, BF16), f(ROPE_LANES, BF16), f(GW, F32), f(GW, BF16),
                 f(QK_ROPE, F32), f(QK_ROPE, BF16), f(GW, F32),
                 jax.ShapeDtypeStruct((CONV_WIDTH - 1, bd, GW), F32),
                 jax.ShapeDtypeStruct((bd, GW), F32),
                 jax.ShapeDtypeStruct((POOL_BUF, bd, GW), F32)]
    args = (x_tm, mod, lw["w_in"], sgu_coef, sgu_bias, lw["vec"], lw["w_gate"], lw["b_gate"],
            lw["w_q"], lw["w_uk"], lw["w_pool"], cos, sin, cbuf, h0, pbuf)
    in_specs = [_full(a.shape) for a in args]
    in_specs[1] = pl.BlockSpec((bd, 2 * d), lambda i: (0, 0))
    return pl.pallas_call(
        functools.partial(_premix_sample_kernel, steps=steps, start=start),
        grid=(1,), in_specs=in_specs,
        out_specs=[_full(s.shape) for s in out_shape], out_shape=out_shape,
        compiler_params=_cparams(("arbitrary",)),
        name="premix_sample",
    )(*args)


def _decode_kernel(pt_ref, ql_ref, qr_ref, nk_ref, nkr_ref, *rest, pages, steps):
    del pt_ref
    ckv_pages = rest[:pages]
    kr_pages = rest[pages:2 * pages]
    o_ref = rest[2 * pages]
    m_ref, l_ref, acc_ref, kb_ref, sb_ref = rest[2 * pages + 1:]
    j = pl.program_id(1)
    rows = N_HEADS * steps
    slot = j & 1

    @pl.when(j == 0)
    def _():
        m_ref[...] = jnp.full((rows, 1), MASK_FLOOR, F32)
        l_ref[...] = jnp.zeros((rows, 1), F32)
        acc_ref[...] = jnp.zeros((rows, GW), F32)
        sb_ref[...] = jnp.full(sb_ref.shape, -jnp.inf, F32)
        kb_ref[1] = jnp.zeros(kb_ref.shape[1:], BF16)

    ql = ql_ref[0]
    qr = qr_ref[0]

    def update(s, values):
        m_old = m_ref[...]
        m_new = jnp.maximum(m_old, jnp.max(s, -1, keepdims=True))
        alpha = jnp.exp(m_old - m_new)
        p = jnp.exp(s - m_new)
        l_ref[...] = alpha * l_ref[...] + jnp.sum(p, -1, keepdims=True)
        acc_ref[...] = alpha * acc_ref[...] + _dot(p.astype(BF16), values)
        m_ref[...] = m_new

    parts = []
    for i in range(pages):
        k = ckv_pages[i][...].astype(BF16)
        kb_ref[slot, i * PAGE_SIZE:(i + 1) * PAGE_SIZE, :] = k
        parts.append(_dot_nt(ql, k) + _dot(qr, kr_pages[i][...].astype(BF16)))
    s_now = jnp.concatenate(parts, -1)
    update(sb_ref[...], kb_ref[1 - slot])
    sb_ref[...] = s_now

    @pl.when(j == pl.num_programs(1) - 1)
    def _():
        update(sb_ref[...], kb_ref[slot])
        nk = nk_ref[0]
        s_new = _dot_nt(ql, nk) + _dot_nt(qr, nkr_ref[0])
        qstep = _mod(lax.broadcasted_iota(jnp.int32, (rows, NEW_KEY_PAD), 0), steps)
        kstep = lax.broadcasted_iota(jnp.int32, (rows, NEW_KEY_PAD), 1)
        update(jnp.where(kstep <= qstep, s_new, -jnp.inf), nk)
        o_ref[0] = acc_ref[...] / l_ref[...]


def _decode_attn(layer, page_table, cache_ckv, cache_krt, qlat, qrope, nk, nkr):
    bd, rows, _ = qlat.shape
    n_pages = page_table.shape[1]
    pages = min(DECODE_PAGES, n_pages)
    assert n_pages % pages == 0
    steps = rows // N_HEADS

    def page_spec(shape, k):
        return pl.BlockSpec((None, None) + shape, lambda i, j, pt: (layer, pt[i, j * pages + k], 0, 0))

    per_seq = lambda r, w: pl.BlockSpec((1, r, w), lambda i, j, pt: (i, 0, 0))
    in_specs = ([per_seq(rows, GW), per_seq(rows, QK_ROPE), per_seq(NEW_KEY_PAD, GW), per_seq(NEW_KEY_PAD, QK_ROPE)]
                + [page_spec((PAGE_SIZE, GW), k) for k in range(pages)]
                + [page_spec((QK_ROPE, PAGE_SIZE), k) for k in range(pages)])
    grid_spec = pltpu.PrefetchScalarGridSpec(
        num_scalar_prefetch=1, grid=(bd, n_pages // pages),
        in_specs=in_specs, out_specs=per_seq(rows, GW),
        scratch_shapes=[pltpu.VMEM((rows, 1), F32), pltpu.VMEM((rows, 1), F32), pltpu.VMEM((rows, GW), F32),
                        pltpu.VMEM((2, pages * PAGE_SIZE, GW), BF16),
                        pltpu.VMEM((rows, pages * PAGE_SIZE), F32)])
    return pl.pallas_call(
        functools.partial(_decode_kernel, pages=pages, steps=steps),
        grid_spec=grid_spec,
        out_shape=jax.ShapeDtypeStruct((bd, rows, GW), F32),
        compiler_params=_cparams(("arbitrary", "arbitrary")),
        name="decode_attn",
    )(page_table, qlat, qrope, nk, nkr, *([cache_ckv] * pages), *([cache_krt] * pages))


def _uv_kernel(o_ref, wuv_ref, out_ref):
    out = _dot(o_ref[0].astype(BF16), wuv_ref[0])
    for hh in range(1, N_HEADS):
        out = out + _dot(o_ref[hh].astype(BF16), wuv_ref[hh])
    out_ref[...] = out.astype(BF16)


def _uv_project(o_heads, wuv):
    _, n, _ = o_heads.shape
    return pl.pallas_call(
        _uv_kernel, grid=(1,),
        in_specs=[_full(o_heads.shape), _full(wuv.shape)],
        out_specs=_full((n, GW)), out_shape=jax.ShapeDtypeStruct((n, GW), BF16),
        compiler_params=_cparams(("arbitrary",)), name="uv_project",
    )(o_heads, wuv)


def _mix_norm(x_ref, abd_ref, c_ref, moda_ref, modb_ref, wabd_ref, wc_ref, lnv_ref):
    mix = _dot(abd_ref[...], wabd_ref[...]) + _dot(c_ref[...], wc_ref[...])
    g1 = moda_ref[0, :, 0:D_MODEL]
    sh2 = moda_ref[0, :, D_MODEL:2 * D_MODEL]
    sc2 = modb_ref[0, :, 0:D_MODEL]
    x1 = _ln(ALPHA * x_ref[...] + g1 * mix) * lnv_ref[0:1, :] + lnv_ref[1:2, :]
    h2 = (_ln(x1) * (1.0 + sc2) + sh2).astype(BF16)
    return x1, h2


def _swiglu(h2, w1, w3, w2):
    return _dot((_silu(_dot(h2, w1)) * _dot(h2, w3)).astype(BF16), w2)


def _post_dense_kernel(x_ref, abd_ref, c_ref, moda_ref, modb_ref, wabd_ref, wc_ref, lnv_ref,
                       w1_ref, w3_ref, w2_ref, o_ref):
    x1, h2 = _mix_norm(x_ref, abd_ref, c_ref, moda_ref, modb_ref, wabd_ref, wc_ref, lnv_ref)
    f = _swiglu(h2, w1_ref[...], w3_ref[...], w2_ref[...])
    g2 = modb_ref[0, :, D_MODEL:2 * D_MODEL]
    o_ref[...] = _ln(ALPHA * x1 + g2 * f) * lnv_ref[2:3, :] + lnv_ref[3:4, :]


def _token_specs(tm, tiles_per_group, mod_rows):
    mi = lambda blk: (lambda i, *_: (i // tiles_per_group, 0, blk))
    tok = lambda w: pl.BlockSpec((tm, w), lambda i, *_: (i, 0))
    mod = lambda blk: pl.BlockSpec((1, mod_rows, 2 * D_MODEL), mi(blk))
    return tok, mod


def _post_dense(x, abd, c, mod, lw, fw, tm, tiles_per_group):
    n, d = x.shape
    tok, modspec = _token_specs(tm, tiles_per_group, mod.shape[1])
    weights = (lw["w_abd"], lw["w_c"], lw["lnv"], fw["w1"], fw["w3"], fw["w2"])
    return pl.pallas_call(
        _post_dense_kernel, grid=(n // tm,),
        in_specs=[tok(d), tok(3 * GW), tok(GW), modspec(1), modspec(2)] + [_full(w.shape) for w in weights],
        out_specs=tok(d), out_shape=jax.ShapeDtypeStruct((n, d), F32),
        compiler_params=_cparams(("arbitrary",)), name="post_dense",
    )(x, abd, c, mod, mod, *weights)


def _post_route_kernel(x_ref, abd_ref, c_ref, moda_ref, modb_ref, wabd_ref, wc_ref, lnv_ref,
                       wr_ref, br_ref, x1_ref, h2_ref, route_ref, cnt_ref, run_ref):
    tm = x_ref.shape[0]

    @pl.when(pl.program_id(0) == 0)
    def _():
        run_ref[...] = jnp.zeros((1, 128), F32)

    x1, h2 = _mix_norm(x_ref, abd_ref, c_ref, moda_ref, modb_ref, wabd_ref, wc_ref, lnv_ref)
    x1_ref[...] = x1
    h2_ref[...] = h2
    lane = lax.broadcasted_iota(jnp.int32, (1, 128), 1)
    logits = jnp.where(lane < N_EXPERTS, _dot(h2, wr_ref[...]) + br_ref[...], -jnp.inf)
    m1 = jnp.max(logits, -1, keepdims=True)
    i1 = jnp.min(jnp.where(logits == m1, lane, 128), -1, keepdims=True)
    rest = jnp.where(lane == i1, -jnp.inf, logits)
    m2 = jnp.max(rest, -1, keepdims=True)
    i2 = jnp.min(jnp.where(rest == m2, lane, 128), -1, keepdims=True)
    e2 = jnp.exp(m2 - m1)
    gate1 = 1.0 / (1.0 + e2)
    gate2 = e2 / (1.0 + e2)
    sel = jnp.where((lane == i1) | (lane == i2), 1.0, 0.0)
    r = lax.broadcasted_iota(jnp.int32, (tm, tm), 0)
    c = lax.broadcasted_iota(jnp.int32, (tm, tm), 1)
    before = _dot(jnp.where(c < r, 1.0, 0.0).astype(BF16), sel.astype(BF16)) + run_ref[...]
    rank1 = jnp.sum(jnp.where(lane == i1, before, 0.0), -1, keepdims=True)
    rank2 = jnp.sum(jnp.where(lane == i2, before, 0.0), -1, keepdims=True)
    run_ref[...] = run_ref[...] + jnp.sum(sel, 0, keepdims=True)
    cnt_ref[...] = run_ref[...]
    cols = (i1.astype(F32), i2.astype(F32), gate1, gate2, rank1, rank2)
    route = jnp.zeros((tm, 128), F32)
    for k, v in enumerate(cols):
        route = jnp.where(lane == k, v, route)
    route_ref[...] = route


def _post_route(x, abd, c, mod, lw, mw, tm, tiles_per_group):
    n, d = x.shape
    tok, modspec = _token_specs(tm, tiles_per_group, mod.shape[1])
    weights = (lw["w_abd"], lw["w_c"], lw["lnv"], mw["w_router"], mw["b_router"])
    return pl.pallas_call(
        _post_route_kernel, grid=(n // tm,),
        in_specs=[tok(d), tok(3 * GW), tok(GW), modspec(1), modspec(2)] + [_full(w.shape) for w in weights],
        out_specs=[tok(d), tok(d), tok(128), _full((1, 128))],
        out_shape=[jax.ShapeDtypeStruct((n, d), F32), jax.ShapeDtypeStruct((n, d), BF16),
                   jax.ShapeDtypeStruct((n, 128), F32), jax.ShapeDtypeStruct((1, 128), F32)],
        scratch_shapes=[pltpu.VMEM((1, 128), F32)],
        compiler_params=_cparams(("arbitrary",)), name="post_route",
    )(x, abd, c, mod, mod, *weights)


def _moe_group_kernel(tile_ref, blk_ref, first_ref, last_ref, valid_ref, texp_ref,
                      h2_ref, post_ref, w1_ref, w3_ref, w2_ref, out_ref, xg_ref):
    del blk_ref, texp_ref
    w = pl.program_id(0)

    @pl.when(first_ref[w] == 1)
    def _():
        xg_ref[...] = jnp.zeros_like(xg_ref)

    @pl.when(valid_ref[w] == 1)
    def _():
        row = tile_ref[w] * MOE_TILE + lax.broadcasted_iota(jnp.int32, (MOE_TILE, 1), 0)
        pick = (post_ref[0:1, :] == row) | (post_ref[1:2, :] == row)
        xg_ref[...] += _dot(jnp.where(pick, 1.0, 0.0).astype(BF16), h2_ref[...])

    @pl.when(last_ref[w] == 1)
    def _():
        out_ref[...] = _swiglu(xg_ref[...].astype(BF16), w1_ref[...], w3_ref[...], w2_ref[...]).astype(BF16)


def _moe_group(items, tile_expert, h2, pos_t, mw, n_rows):
    n, d = h2.shape
    n_items = items[0].shape[0]
    wspec = lambda shape: pl.BlockSpec((None,) + shape, lambda w, tile, blk, f, l, v, te: (te[tile[w]], 0, 0))
    grid_spec = pltpu.PrefetchScalarGridSpec(
        num_scalar_prefetch=6, grid=(n_items,),
        in_specs=[pl.BlockSpec((MOE_BLOCK, d), lambda w, tile, blk, f, l, v, te: (blk[w], 0)),
                  pl.BlockSpec((2, MOE_BLOCK), lambda w, tile, blk, f, l, v, te: (0, blk[w])),
                  wspec((d, D_FF)), wspec((d, D_FF)), wspec((D_FF, d))],
        out_specs=pl.BlockSpec((MOE_TILE, d), lambda w, tile, blk, f, l, v, te: (tile[w], 0)),
        scratch_shapes=[pltpu.VMEM((MOE_TILE, d), F32)])
    return pl.pallas_call(
        _moe_group_kernel, grid_spec=grid_spec,
        out_shape=jax.ShapeDtypeStruct((n_rows, d), BF16),
        compiler_params=_cparams(("arbitrary",)), name="moe_group",
    )(*items, tile_expert, h2, pos_t, mw["w1"], mw["w3"], mw["w2"])


def _moe_combine_kernel(blk_ref, tile_ref, first_ref, last_ref, valid_ref,
                        out_ref, pos_ref, gate_ref, y_ref, acc_ref):
    del blk_ref
    w = pl.program_id(0)

    @pl.when(first_ref[w] == 1)
    def _():
        acc_ref[...] = jnp.zeros_like(acc_ref)

    @pl.when(valid_ref[w] == 1)
    def _():
        row = tile_ref[w] * MOE_TILE + lax.broadcasted_iota(jnp.int32, (1, MOE_TILE), 1)
        sel = (jnp.where(pos_ref[:, 0:1] == row, gate_ref[:, 0:1], 0.0)
               + jnp.where(pos_ref[:, 1:2] == row, gate_ref[:, 1:2], 0.0))
        acc_ref[...] += _dot(sel.astype(BF16), out_ref[...])

    @pl.when(last_ref[w] == 1)
    def _():
        y_ref[...] = acc_ref[...]


def _moe_combine(items, out_sorted, pos, gates):
    n = pos.shape[0]
    d = out_sorted.shape[1]
    n_items = items[0].shape[0]
    tokb = lambda w_: pl.BlockSpec((MOE_BLOCK, w_), lambda w, blk, tile, f, l, v: (blk[w], 0))
    grid_spec = pltpu.PrefetchScalarGridSpec(
        num_scalar_prefetch=5, grid=(n_items,),
        in_specs=[pl.BlockSpec((MOE_TILE, d), lambda w, blk, tile, f, l, v: (tile[w], 0)), tokb(2), tokb(2)],
        out_specs=tokb(d),
        scratch_shapes=[pltpu.VMEM((MOE_BLOCK, d), F32)])
    return pl.pallas_call(
        _moe_combine_kernel, grid_spec=grid_spec,
        out_shape=jax.ShapeDtypeStruct((n, d), F32),
        compiler_params=_cparams(("arbitrary",)), name="moe_combine",
    )(*items, out_sorted, pos, gates)


def _moe_final_kernel(x1_ref, y_ref, modb_ref, lnv_ref, o_ref):
    g2 = modb_ref[0, :, D_MODEL:2 * D_MODEL]
    o_ref[...] = _ln(ALPHA * x1_ref[...] + g2 * y_ref[...]) * lnv_ref[2:3, :] + lnv_ref[3:4, :]


def _moe_final(x1, y, y_row0, mod, lw, tm, tiles_per_group):
    n, d = x1.shape
    assert y_row0 % tm == 0
    tok, modspec = _token_specs(tm, tiles_per_group, mod.shape[1])
    return pl.pallas_call(
        _moe_final_kernel, grid=(n // tm,),
        in_specs=[tok(d), pl.BlockSpec((tm, d), lambda i: (i + y_row0 // tm, 0)), modspec(2),
                  _full(lw["lnv"].shape)],
        out_specs=tok(d), out_shape=jax.ShapeDtypeStruct((n, d), F32),
        compiler_params=_cparams(("arbitrary",)), name="moe_final",
    )(x1, y, mod, lw["lnv"])


def _pair_items(incidence, n_items):
    rows, cols = jnp.nonzero(incidence, size=n_items, fill_value=0)
    n_valid = jnp.sum(incidence.astype(jnp.int32))
    idx = jnp.arange(n_items)
    valid = idx < n_valid
    last_valid = jnp.maximum(n_valid - 1, 0)
    rows = jnp.where(valid, rows, rows[last_valid]).astype(jnp.int32)
    cols = jnp.where(valid, cols, cols[last_valid]).astype(jnp.int32)
    prev_rows = jnp.concatenate([rows[:1] - 1, rows[:-1]])
    next_rows = jnp.concatenate([rows[1:], rows[-1:] + 1])
    first = valid & (rows != prev_rows)
    last = valid & ((rows != next_rows) | (idx == last_valid))
    return rows, cols, first.astype(jnp.int32), last.astype(jnp.int32), valid.astype(jnp.int32)


def _moe_ffn(h2, route, counts, mw):
    n, d = h2.shape
    assert n % MOE_BLOCK == 0
    n_blocks = n // MOE_BLOCK
    n_tiles = (2 * n + N_EXPERTS * (MOE_TILE - 1)) // MOE_TILE + 1
    n_items = n_tiles + N_EXPERTS * n_blocks
    experts = route[:, 0:2].astype(jnp.int32)
    gates = route[:, 2:4]
    tiles_per_expert = (counts.astype(jnp.int32) + MOE_TILE - 1) // MOE_TILE
    tile_end = jnp.cumsum(tiles_per_expert)
    row_start = (tile_end - tiles_per_expert) * MOE_TILE
    pos = row_start[experts] + route[:, 4:6].astype(jnp.int32)
    tile_expert = jnp.minimum(jnp.searchsorted(tile_end, jnp.arange(n_tiles), side="right"),
                              N_EXPERTS - 1).astype(jnp.int32)
    pick_tile = (pos // MOE_TILE).reshape(n_blocks, 1, 2 * MOE_BLOCK)
    incidence = jnp.any(pick_tile == jnp.arange(n_tiles)[None, :, None], -1)
    tile_i, blk_i, first_i, last_i, valid_i = _pair_items(incidence.T, n_items)
    out_sorted = _moe_group((tile_i, blk_i, first_i, last_i, valid_i), tile_expert, h2, pos.T, mw,
                            n_tiles * MOE_TILE)
    return _moe_combine(_pair_items(incidence, n_items), out_sorted, pos, gates)


def _block_diag(blocks):
    n, r, c = blocks.shape
    out = jnp.zeros((n * r, n * c), blocks.dtype)
    for i in range(n):
        out = out.at[i * r:(i + 1) * r, i * c:(i + 1) * c].set(blocks[i])
    return out


def _swap_rope_halves(w):
    half = QK_ROPE // 2
    return jnp.concatenate([w[..., half:], w[..., :half]], -1)


def _layer_weights(l, p):
    w_in = p["w_in"][l]
    s = np.cumsum([GW, GW, GW, GW, GW, GW, QK_ROPE])
    kr_cols = w_in[:, s[5]:s[6]]
    gap = jnp.zeros_like(kr_cols)
    kr_sw = _swap_rope_halves(kr_cols)
    w_in_r = jnp.concatenate(
        [w_in[:, :s[5]], w_in[:, s[6]:], kr_cols, gap, kr_cols, gap, kr_sw, gap, kr_sw, gap], 1).astype(BF16)
    vec = jnp.zeros((VEC_ROWS, GW), F32)
    for i, name in ((V_SGU_G, "sgu_ln_g"), (V_SGU_B, "sgu_ln_b"), (V_CONV_B, "lru_conv_b"), (V_LAM, "lru_lambda"),
                    (V_QNORM, "mla_q_norm"), (V_KVNORM, "mla_kv_norm"), (V_PSCALE, "pool_scale")):
        vec = vec.at[i].set(p[name][l])
    vec = vec.at[V_CONV_W:V_CONV_W + CONV_WIDTH].set(p["lru_conv_w"][l])
    w_uq = p["mla_w_uq"][l]
    q_rope_cols = w_uq[:, :, QK_NOPE:]
    w_q = jnp.concatenate([w_uq[:, :, :QK_NOPE].reshape(GW, GW), q_rope_cols.reshape(GW, ROPE_LANES),
                           _swap_rope_halves(q_rope_cols).reshape(GW, ROPE_LANES)], 1).astype(BF16)
    w_uk = _block_diag(jnp.transpose(p["mla_w_uk"][l], (1, 2, 0))).astype(BF16)
    w_uv = p["mla_w_uv"][l]
    zq = jnp.zeros((GW, N_HEADS, HEAD_LANES - QK_NOPE - QK_ROPE), F32)
    w_qh = jnp.concatenate(
        [jnp.concatenate([w_uq, zq], -1).reshape(GW, N_HEADS * HEAD_LANES),
         jnp.concatenate([jnp.zeros((GW, N_HEADS, QK_NOPE), F32), _swap_rope_halves(q_rope_cols), zq],
                         -1).reshape(GW, N_HEADS * HEAD_LANES)], 1).astype(BF16)
    w_kn = jnp.concatenate([p["mla_w_uk"][l], jnp.zeros((GW, N_HEADS, HEAD_LANES - QK_NOPE), F32)], -1)
    w_kv = jnp.concatenate([w_kn.reshape(GW, N_HEADS * HEAD_LANES), w_uv.reshape(GW, GW)], 1).astype(BF16)
    w_uv_pad = jnp.zeros((N_HEADS, GW, GW), F32)
    for hh in range(N_HEADS):
        w_uv_pad = w_uv_pad.at[hh, :, hh * HEAD_DIM:(hh + 1) * HEAD_DIM].set(w_uv[:, hh, :])
    w_out = p["w_out"][l]
    return {
        "w_in": w_in_r,
        "sgu_w": p["sgu_w"][l],
        "sgu_b": p["sgu_b"][l],
        "vec": vec,
        "w_gate": jnp.concatenate([_block_diag(p["lru_wa"][l]), _block_diag(p["lru_wx"][l])], 1).astype(BF16),
        "b_gate": jnp.concatenate([p["lru_ba"][l], p["lru_bx"][l]])[None, :],
        "w_q": w_q,
        "w_uk": w_uk,
        "w_qh": w_qh,
        "w_kv": w_kv,
        "w_uv": w_uv_pad.astype(BF16),
        "w_pool": _block_diag(p["pool_w"][l]).astype(BF16),
        "w_abd": jnp.concatenate([w_out[:2 * GW], w_out[3 * GW:]], 0).astype(BF16),
        "w_c": w_out[2 * GW:3 * GW].astype(BF16),
        "lnv": jnp.stack([p["ln1_g"][l], p["ln1_b"][l], p["ln2_g"][l], p["ln2_b"][l]]),
    }


def _rope_tables(start, steps):
    half = QK_ROPE // 2
    freq = ROPE_THETA ** (-jnp.arange(half, dtype=F32) / half)
    ang = (start + jnp.arange(steps)).astype(F32)[:, None] * freq[None, :]
    cos = jnp.cos(ang)
    sin = jnp.sin(ang)
    return jnp.concatenate([cos, cos], -1), jnp.concatenate([-sin, sin], -1)


def _rope_tiled(start, steps):
    cos, sin = _rope_tables(start, steps)
    return jnp.tile(cos, (1, N_HEADS)), jnp.tile(sin, (1, N_HEADS))


def _rope_head_tables(start, steps):
    cos, sin = _rope_tables(start, steps)
    one = jnp.ones((steps, QK_NOPE), F32)
    z64 = jnp.zeros((steps, QK_NOPE), F32)
    z32 = jnp.zeros((steps, QK_ROPE), F32)
    return jnp.concatenate([one, cos, z32, z64, sin, z32, cos, z32, cos, z32, sin, z32, sin, z32], 1)


def kernel(x_prompt, x_sample, c_prompt, c_sample, cache_ckv, cache_kr, page_table, state_lru_conv, state_lru_h, state_pool, ada_w, ada_b, w_in, w_out, ln1_g, ln1_b, ln2_g, ln2_b, sgu_ln_g, sgu_ln_b, sgu_w, sgu_b, lru_conv_w, lru_conv_b, lru_wa, lru_ba, lru_wx, lru_bx, lru_lambda, mla_q_norm, mla_w_uq, mla_kv_norm, mla_w_uk, mla_w_uv, pool_w, pool_scale, ffn_w1, ffn_w3, ffn_w2, moe_router, moe_router_b, moe_w1, moe_w3, moe_w2):
    p = dict(w_in=w_in, w_out=w_out, ln1_g=ln1_g, ln1_b=ln1_b, ln2_g=ln2_g, ln2_b=ln2_b,
             sgu_ln_g=sgu_ln_g, sgu_ln_b=sgu_ln_b, sgu_w=sgu_w, sgu_b=sgu_b, lru_conv_w=lru_conv_w,
             lru_conv_b=lru_conv_b, lru_wa=lru_wa, lru_ba=lru_ba, lru_wx=lru_wx, lru_bx=lru_bx,
             lru_lambda=lru_lambda, mla_q_norm=mla_q_norm, mla_w_uq=mla_w_uq, mla_kv_norm=mla_kv_norm,
             mla_w_uk=mla_w_uk, mla_w_uv=mla_w_uv, pool_w=pool_w, pool_scale=pool_scale)
    bp, tp, d = x_prompt.shape
    bd, td, _ = x_sample.shape
    n_pages = page_table.shape[1]
    past_len = n_pages * PAGE_SIZE
    tmp = PROMPT_TILE

    rope_p = _rope_head_tables(0, tp)
    cos_s, sin_s = _rope_tiled(past_len, td)
    cache_krt = jnp.swapaxes(cache_kr, 2, 3)
    pad = (-(bd + bp)) % 8
    c_all = jnp.concatenate([c_sample, c_prompt, jnp.zeros((pad, d), F32)], 0)

    xp = x_prompt
    xs = jnp.transpose(x_sample, (1, 0, 2))
    sp = [[] for _ in range(5)]
    ss = [[] for _ in range(6)]
    for l in range(DEPTH):
        lw = _layer_weights(l, p)
        mod = _ada(c_all, ada_w[l].astype(BF16), ada_b[l][None, :])
        mod_s = mod[:bd]
        mod_p = mod[bd:bd + bp][:, None, :]

        (abd_p, qcat_p, kcat_p, vt_p, ckv_p, kr_p, nconv_p, nh_p, npool_p) = _premix_prompt(
            xp, mod_p, dict(lw, sgu_w=lw["sgu_w"][:, :CHUNK, :CHUNK],
                            sgu_bias=jnp.repeat(lw["sgu_b"][:, :CHUNK].T, HEAD_DIM, axis=1)), rope_p)
        c_p = _attn_prompt(qcat_p, kcat_p, vt_p)

        sgu_coef = jnp.repeat(jnp.transpose(lw["sgu_w"][:, :td, :td], (1, 2, 0)), HEAD_DIM, axis=2)
        sgu_bias_s = jnp.repeat(lw["sgu_b"][:, :td].T, HEAD_DIM, axis=1)
        (abd_s, qlat_s, qrope_s, ckv_s, ckvb_s, kr_s, krb_s, vrows_s, nconv_s, nh_s, npool_s) = _premix_sample(
            xs, mod_s, lw, sgu_coef, sgu_bias_s, cos_s, sin_s,
            jnp.transpose(state_lru_conv[l], (1, 0, 2)), state_lru_h[l],
            jnp.transpose(state_pool[l], (1, 0, 2)), past_len)
        to_rows = lambda a, w: jnp.transpose(a.reshape(td, bd, N_HEADS, w), (1, 2, 0, 3)).reshape(bd, N_HEADS * td, w)
        pad_keys = lambda a: jnp.pad(jnp.transpose(a, (1, 0, 2)), ((0, 0), (0, NEW_KEY_PAD - td), (0, 0)))
        o_s = _decode_attn(l, page_table, cache_ckv, cache_krt, to_rows(qlat_s, GW), to_rows(qrope_s, QK_ROPE),
                           pad_keys(ckvb_s), pad_keys(krb_s))
        o_heads = jnp.transpose(o_s.reshape(bd, N_HEADS, td, GW), (1, 2, 0, 3)).reshape(N_HEADS, td * bd, GW)
        c_s = _uv_project(o_heads, lw["w_uv"])

        xp2 = xp.reshape(bp * tp, d)
        xs2 = xs.reshape(td * bd, d)
        args_p = (xp2, abd_p.reshape(bp * tp, 3 * GW), c_p.reshape(bp * tp, GW), mod_p)
        args_s = (xs2, abd_s.reshape(td * bd, 3 * GW), c_s, mod_s[None])
        j = l // 2
        if l % 2 == 0:
            fw = {"w1": ffn_w1[j].astype(BF16), "w3": ffn_w3[j].astype(BF16), "w2": ffn_w2[j].astype(BF16)}
            xp2 = _post_dense(*args_p, lw, fw, tmp, tp // tmp)
            xs2 = _post_dense(*args_s, lw, fw, bd, td)
        else:
            mw = {"w_router": jnp.pad(moe_router[j], ((0, 0), (0, 128 - N_EXPERTS))).astype(BF16),
                  "b_router": jnp.pad(moe_router_b[j], (0, 128 - N_EXPERTS))[None, :],
                  "w1": moe_w1[j].astype(BF16), "w3": moe_w3[j].astype(BF16), "w2": moe_w2[j].astype(BF16)}
            x1p, h2p, route_p, cnt_p = _post_route(*args_p, lw, mw, tmp, tp // tmp)
            x1s, h2s, route_s, cnt_s = _post_route(*args_s, lw, mw, bd, td)
            before_s = cnt_p[0, :N_EXPERTS][route_s[:, 0:2].astype(jnp.int32)]
            route = jnp.concatenate([route_p, route_s.at[:, 4:6].add(before_s)], 0)
            y = _moe_ffn(jnp.concatenate([h2p, h2s], 0), route, (cnt_p + cnt_s)[0, :N_EXPERTS], mw)
            xp2 = _moe_final(x1p, y, 0, mod_p, lw, tmp, tp // tmp)
            xs2 = _moe_final(x1s, y, bp * tp, mod_s[None], lw, bd, td)
        xp = xp2.reshape(bp, tp, d)
        xs = xs2.reshape(td, bd, d)

        for k, v in enumerate((ckv_p, kr_p, nconv_p, nh_p[:, 0], npool_p)):
            sp[k].append(v)
        bt = lambda a: jnp.transpose(a, (1, 0, 2))
        for k, v in enumerate((bt(ckv_s), bt(kr_s), bt(nconv_s), nh_s, bt(npool_s), bt(vrows_s))):
            ss[k].append(v)

    return (xp, jnp.transpose(xs, (1, 0, 2)),
            jnp.stack(sp[0]), jnp.stack(sp[1]), jnp.stack(sp[2]), jnp.stack(sp[3]), jnp.stack(sp[4]),
            jnp.stack(ss[0]), jnp.stack(ss[1]), jnp.stack(ss[2]), jnp.stack(ss[3]), jnp.stack(ss[4]),
            jnp.stack(ss[5]))
```

```python
import functools

import jax
import jax.numpy as jnp
import numpy as np
from jax import lax
from jax.experimental import pallas as pl
from jax.experimental.pallas import tpu as pltpu

F32 = jnp.float32
BF16 = jnp.bfloat16

D_MODEL = 1024
DEPTH = 2
PAGE_SIZE = 128
GROUP_WIDTH = D_MODEL // 4
HEAD_DIM = 64
N_HEADS = GROUP_WIDTH // HEAD_DIM
CHUNK = 128
CONV_WIDTH = 4
LRU_C = 8.0
QK_NOPE = HEAD_DIM
QK_ROPE = HEAD_DIM // 2
MLA_SCALE = (QK_NOPE + QK_ROPE) ** -0.5
ROPE_THETA = 10000.0
POOL_WINDOWS = (2, 4, 8, 16)
POOL_BUF = 15
POOL_GC = GROUP_WIDTH // 4
D_FF = ((8 * D_MODEL // 3 + 127) // 128) * 128
N_EXPERTS = 8
ALPHA = (2 * DEPTH) ** 0.25
LN_EPS = 1e-5
RMS_EPS = 1e-6

GW = GROUP_WIDTH
OFF_U, OFF_V, OFF_XB, OFF_GB, OFF_CQ, OFF_CKV, OFF_PD, OFF_KRA, OFF_KRB = (
    0, GW, 2 * GW, 3 * GW, 4 * GW, 5 * GW, 6 * GW, 7 * GW, 7 * GW + 128)
P_IN_PAD = 8 * GW
ROPE_LANES = N_HEADS * QK_ROPE
HEAD_LANES = 128

V_SGU_G, V_SGU_B, V_CONV_B, V_LAM, V_QNORM, V_KVNORM, V_PSCALE = 0, 1, 2, 3, 4, 5, 6
V_CONV_W = 8
VEC_ROWS = 16

VMEM_LIMIT_BYTES = 56 * 1024 * 1024
PROMPT_TILE = 256
ATTN_TILE = 256
DECODE_PAGES = 16
NEW_KEY_PAD = 16
MASK_FLOOR = -3.0e38
MOE_TILE = 256
MOE_BLOCK = 512


def _dot(a, b):
    return jnp.dot(a, b, preferred_element_type=F32)


def _dot_nt(a, b):
    return lax.dot_general(a, b, (((1,), (1,)), ((), ())), preferred_element_type=F32)


def _ln(x):
    mu = jnp.mean(x, -1, keepdims=True)
    xc = x - mu
    var = jnp.mean(xc * xc, -1, keepdims=True)
    return xc * lax.rsqrt(var + LN_EPS)


def _rms(x):
    return x * lax.rsqrt(jnp.mean(x * x, -1, keepdims=True) + RMS_EPS)


def _gelu(x):
    return 0.5 * x * (1.0 + jnp.tanh(0.7978845608028654 * (x + 0.044715 * (x * x * x))))


def _silu(x):
    return x * jax.nn.sigmoid(x)


def _softplus(x):
    return jnp.maximum(x, 0.0) + jnp.log1p(jnp.exp(-jnp.abs(x)))


def _lane_group(width, group):
    assert group & (group - 1) == 0
    return lax.broadcasted_iota(jnp.int32, (1, width), 1) >> (group.bit_length() - 1)


def _mod(x, n):
    return x & (n - 1) if n & (n - 1) == 0 else lax.rem(x, n)


def _cparams(sem):
    return pltpu.CompilerParams(dimension_semantics=sem, vmem_limit_bytes=VMEM_LIMIT_BYTES)


def _full(shape):
    n = len(shape)
    return pl.BlockSpec(shape, lambda *_: (0,) * n)


def _ada_kernel(c_ref, w_ref, b_ref, o_ref):
    o_ref[...] = _dot(_silu(c_ref[...]).astype(BF16), w_ref[...]) + b_ref[...]


def _ada(c_all, w, b):
    m, d = c_all.shape
    n = w.shape[1]
    tn = 512
    return pl.pallas_call(
        _ada_kernel,
        grid=(n // tn,),
        in_specs=[pl.BlockSpec((m, d), lambda j: (0, 0)),
                  pl.BlockSpec((d, tn), lambda j: (0, j)),
                  pl.BlockSpec((1, tn), lambda j: (0, j))],
        out_specs=pl.BlockSpec((m, tn), lambda j: (0, j)),
        out_shape=jax.ShapeDtypeStruct((m, n), F32),
        compiler_params=_cparams(("arbitrary",)),
        name="ada",
    )(c_all, w, b)


def _lru_gates(xc, wgate, bgate, lam):
    g = _dot(xc.astype(BF16), wgate) + bgate
    r = jax.nn.sigmoid(g[:, :GW])
    i = jax.nn.sigmoid(g[:, GW:])
    log_a = (-LRU_C) * r * _softplus(-lam)
    a = jnp.exp(log_a)
    bt = jnp.sqrt(-jnp.tanh(log_a) * (a * a + 1.0)) * i * xc
    return a, bt


def _mla_queries(cq, qnorm, wq, wuk, cos, sin):
    q = _dot((_rms(cq) * qnorm).astype(BF16), wq)
    qlat = _dot(q[:, :GW].astype(BF16), wuk) * MLA_SCALE
    qr = (q[:, GW:GW + ROPE_LANES] * cos + q[:, GW + ROPE_LANES:] * sin) * MLA_SCALE
    return qlat, qr


def _pool_select(s2, s4, s8, s16):
    grp = _lane_group(GW, POOL_GC)
    return jnp.where(grp == 0, s2, jnp.where(grp == 1, s4, jnp.where(grp == 2, s8, s16)))


def _pool_windows():
    grp = _lane_group(GW, POOL_GC)
    w = jnp.where(grp == 0, POOL_WINDOWS[0],
                  jnp.where(grp == 1, POOL_WINDOWS[1],
                            jnp.where(grp == 2, POOL_WINDOWS[2], POOL_WINDOWS[3])))
    return w.astype(F32)


def _premix_prompt_kernel(x_ref, mod_ref, win_ref, sguw_ref, sgub_ref, vec_ref, wgate_ref, bgate_ref,
                          wq_ref, wkv_ref, wpool_ref, rope_ref,
                          abd_ref, qcat_ref, kcat_ref, vt_ref, ckvf_ref, krf_ref,
                          nconv_ref, nh_ref, npool_ref,
                          zc_ref, zp_ref, hc_ref, *, tm):
    t = pl.program_id(1)

    @pl.when(t == 0)
    def _():
        zc_ref[0:8, :] = jnp.zeros((8, GW), F32)
        zp_ref[0:16, :] = jnp.zeros((16, GW), F32)
        hc_ref[...] = jnp.zeros((1, GW), F32)

    def vec(i):
        return vec_ref[i:i + 1, :]

    x = x_ref[0]
    sh1 = mod_ref[0, :, 0:D_MODEL]
    sc1 = mod_ref[0, :, D_MODEL:2 * D_MODEL]
    h = _ln(x) * (1.0 + sc1) + sh1
    proj = _dot(h.astype(BF16), win_ref[...])

    ug = _gelu(proj[:, OFF_U:OFF_U + GW])
    vg = _ln(_gelu(proj[:, OFF_V:OFF_V + GW])) * vec(V_SGU_G) + vec(V_SGU_B)
    vgb = vg.astype(BF16)
    head = _lane_group(GW, HEAD_DIM)
    row = lax.broadcasted_iota(jnp.int32, (CHUNK, CHUNK), 0)
    col = lax.broadcasted_iota(jnp.int32, (CHUNK, CHUNK), 1)
    wmix = [jnp.where(col <= row, sguw_ref[hh], 0.0).astype(BF16) for hh in range(N_HEADS)]
    parts = []
    for c in range(tm // CHUNK):
        vc = vgb[c * CHUNK:(c + 1) * CHUNK]
        acc = _dot(wmix[0], vc)
        for hh in range(1, N_HEADS):
            acc = jnp.where(head == hh, _dot(wmix[hh], vc), acc)
        parts.append(acc + sgub_ref[...])
    mixed = jnp.concatenate(parts, 0) if len(parts) > 1 else parts[0]
    abd_ref[0, :, 0:GW] = (ug * mixed).astype(BF16)

    xb = proj[:, OFF_XB:OFF_XB + GW]
    zc_ref[8:8 + tm, :] = xb
    z = zc_ref[...]
    xc = (vec(V_CONV_B) + vec(V_CONV_W + 3) * z + vec(V_CONV_W + 2) * pltpu.roll(z, 1, 0)
          + vec(V_CONV_W + 1) * pltpu.roll(z, 2, 0) + vec(V_CONV_W) * pltpu.roll(z, 3, 0))[8:]
    nconv_ref[0] = zc_ref[pl.ds(8 + tm - (CONV_WIDTH - 1), CONV_WIDTH - 1), :]
    zc_ref[0:8, :] = xb[tm - 8:tm]
    a, bt = _lru_gates(xc, wgate_ref[...], bgate_ref[...], vec(V_LAM))
    rowi = lax.broadcasted_iota(jnp.int32, (tm, GW), 0)
    k = 1
    while k < tm:
        keep = rowi >= k
        bt = jnp.where(keep, a * pltpu.roll(bt, k, 0) + bt, bt)
        a = jnp.where(keep, a * pltpu.roll(a, k, 0), a)
        k *= 2
    hs = a * hc_ref[...] + bt
    hc_ref[...] = hs[tm - 1:tm]
    nh_ref[0] = hs[tm - 1:tm]
    abd_ref[0, :, GW:2 * GW] = (hs * _gelu(proj[:, OFF_GB:OFF_GB + GW])).astype(BF16)

    cosq = rope_ref[:, 0:HEAD_LANES]
    sinq = rope_ref[:, HEAD_LANES:2 * HEAD_LANES]
    cosk = rope_ref[:, 2 * HEAD_LANES:3 * HEAD_LANES]
    sink = rope_ref[:, 3 * HEAD_LANES:4 * HEAD_LANES]
    q = _dot((_rms(proj[:, OFF_CQ:OFF_CQ + GW]) * vec(V_QNORM)).astype(BF16), wq_ref[...])
    half = N_HEADS * HEAD_LANES
    for hh in range(N_HEADS):
        lo = hh * HEAD_LANES
        qh = q[:, lo:lo + HEAD_LANES] * cosq + q[:, half + lo:half + lo + HEAD_LANES] * sinq
        qcat_ref[0, hh] = (qh * MLA_SCALE).astype(BF16)
    ckv = _rms(proj[:, OFF_CKV:OFF_CKV + GW]) * vec(V_KVNORM)
    ckvf_ref[0] = ckv
    kv = _dot(ckv.astype(BF16), wkv_ref[...])
    kr = proj[:, OFF_KRA:OFF_KRA + HEAD_LANES] * cosk + proj[:, OFF_KRB:OFF_KRB + HEAD_LANES] * sink
    krf_ref[0] = kr[:, 0:QK_ROPE]
    kr_hi = jnp.where(lax.broadcasted_iota(jnp.int32, (1, HEAD_LANES), 1) >= QK_NOPE, kr, 0.0)
    for hh in range(N_HEADS):
        kcat_ref[0, hh] = (kv[:, hh * HEAD_LANES:(hh + 1) * HEAD_LANES] + kr_hi).astype(BF16)
    vt_ref[0, 0] = kv[:, half:half + GW].T.astype(BF16)

    pd = proj[:, OFF_PD:OFF_PD + GW]
    zp_ref[16:16 + tm, :] = pd
    zz = zp_ref[...]
    s2 = zz + pltpu.roll(zz, 1, 0)
    s4 = s2 + pltpu.roll(s2, 2, 0)
    s8 = s4 + pltpu.roll(s4, 4, 0)
    s16 = s8 + pltpu.roll(s8, 8, 0)
    wsum = _pool_select(s2, s4, s8, s16)[16:]
    pos = (t * tm + lax.broadcasted_iota(jnp.int32, (tm, 1), 0) + 1).astype(F32)
    cnt = jnp.minimum(pos, _pool_windows())
    d = (wsum / cnt - pd).astype(BF16)
    abd_ref[0, :, 2 * GW:3 * GW] = (_dot(d, wpool_ref[...]) * vec(V_PSCALE)).astype(BF16)
    npool_ref[0] = zp_ref[pl.ds(16 + tm - POOL_BUF, POOL_BUF), :]
    zp_ref[0:16, :] = pd[tm - 16:tm]


def _premix_prompt(x, mod, lw, rope):
    b, t, d = x.shape
    tm = PROMPT_TILE
    assert t % tm == 0 and tm % CHUNK == 0 and tm == ATTN_TILE
    grid = (b, t // tm)
    tok = lambda w: pl.BlockSpec((1, tm, w), lambda i, j: (i, j, 0))
    per_b = lambda r: pl.BlockSpec((1, r, GW), lambda i, j: (i, 0, 0))
    heads = pl.BlockSpec((1, N_HEADS, tm, HEAD_LANES), lambda i, j: (i, 0, j, 0))
    in_specs = [
        tok(d),
        pl.BlockSpec((1, 1, 2 * d), lambda i, j: (i, 0, 0)),
        _full(lw["w_in"].shape), _full(lw["sgu_w"].shape), _full(lw["sgu_bias"].shape),
        _full(lw["vec"].shape), _full(lw["w_gate"].shape), _full(lw["b_gate"].shape),
        _full(lw["w_qh"].shape), _full(lw["w_kv"].shape), _full(lw["w_pool"].shape),
        pl.BlockSpec((tm, 4 * HEAD_LANES), lambda i, j: (j, 0)),
    ]
    out_specs = [
        tok(3 * GW), heads, heads,
        pl.BlockSpec((1, 1, GW, tm), lambda i, j: (i, j, 0, 0)),
        tok(GW), tok(QK_ROPE),
        per_b(CONV_WIDTH - 1), per_b(1), per_b(POOL_BUF),
    ]
    out_shape = [
        jax.ShapeDtypeStruct((b, t, 3 * GW), BF16),
        jax.ShapeDtypeStruct((b, N_HEADS, t, HEAD_LANES), BF16),
        jax.ShapeDtypeStruct((b, N_HEADS, t, HEAD_LANES), BF16),
        jax.ShapeDtypeStruct((b, t // tm, GW, tm), BF16),
        jax.ShapeDtypeStruct((b, t, GW), F32),
        jax.ShapeDtypeStruct((b, t, QK_ROPE), F32),
        jax.ShapeDtypeStruct((b, CONV_WIDTH - 1, GW), F32),
        jax.ShapeDtypeStruct((b, 1, GW), F32),
        jax.ShapeDtypeStruct((b, POOL_BUF, GW), F32),
    ]
    return pl.pallas_call(
        functools.partial(_premix_prompt_kernel, tm=tm),
        grid=grid, in_specs=in_specs, out_specs=out_specs, out_shape=out_shape,
        scratch_shapes=[pltpu.VMEM((tm + 8, GW), F32), pltpu.VMEM((tm + 16, GW), F32),
                        pltpu.VMEM((1, GW), F32)],
        compiler_params=_cparams(("arbitrary", "arbitrary")),
        name="premix_prompt",
    )(x, mod, lw["w_in"], lw["sgu_w"], lw["sgu_bias"], lw["vec"], lw["w_gate"], lw["b_gate"],
      lw["w_qh"], lw["w_kv"], lw["w_pool"], rope)


def _attn_prompt_kernel(q_ref, k_ref, vt_ref, o_ref, m_ref, l_ref, acc_ref, s_ref, *, tq):
    qi = pl.program_id(1)
    cols = N_HEADS * tq
    m_ref[...] = jnp.full((1, cols), -jnp.inf, F32)
    l_ref[...] = jnp.zeros((1, cols), F32)
    acc_ref[...] = jnp.zeros((GW, tq), F32)
    qs = [q_ref[0, hh] for hh in range(N_HEADS)]

    def scores(j):
        start = pl.multiple_of(j * tq, tq)
        return jnp.concatenate(
            [_dot_nt(k_ref[0, hh, pl.ds(start, tq), :], qs[hh]) for hh in range(N_HEADS)], 1)

    def accumulate(j, st):
        m_old = m_ref[...]
        m_new = jnp.maximum(m_old, jnp.max(st, 0, keepdims=True))
        alpha = jnp.exp(m_old - m_new)
        p = jnp.exp(st - m_new)
        l_ref[...] = alpha * l_ref[...] + jnp.sum(p, 0, keepdims=True)
        m_ref[...] = m_new
        pb = p.astype(BF16)
        vt = vt_ref[0, j]
        for hh in range(N_HEADS):
            r = slice(hh * HEAD_DIM, (hh + 1) * HEAD_DIM)
            c = slice(hh * tq, (hh + 1) * tq)
            acc_ref[r, :] = alpha[:, c] * acc_ref[r, :] + _dot(vt[r, :], pb[:, c])

    s_ref[...] = scores(0)

    def body(j, carry):
        st = s_ref[...]
        s_next = scores(j + 1)
        accumulate(j, st)
        s_ref[...] = s_next
        return carry

    lax.fori_loop(0, qi, body, 0)
    kpos = lax.broadcasted_iota(jnp.int32, (tq, cols), 0)
    qpos = _mod(lax.broadcasted_iota(jnp.int32, (tq, cols), 1), tq)
    accumulate(qi, jnp.where(kpos <= qpos, s_ref[...], -jnp.inf))
    linv = 1.0 / l_ref[...]
    out = jnp.concatenate(
        [acc_ref[hh * HEAD_DIM:(hh + 1) * HEAD_DIM, :] * linv[:, hh * tq:(hh + 1) * tq] for hh in range(N_HEADS)], 0)
    o_ref[0] = out.T.astype(BF16)


def _attn_prompt(qcat, kcat, vt):
    b, _, t, _ = qcat.shape
    tq = ATTN_TILE
    assert t % tq == 0
    return pl.pallas_call(
        functools.partial(_attn_prompt_kernel, tq=tq),
        grid=(b, t // tq),
        in_specs=[pl.BlockSpec((1, N_HEADS, tq, HEAD_LANES), lambda i, j: (i, 0, j, 0)),
                  pl.BlockSpec((1, N_HEADS, t, HEAD_LANES), lambda i, j: (i, 0, 0, 0)),
                  pl.BlockSpec((1, t // tq, GW, tq), lambda i, j: (i, 0, 0, 0))],
        out_specs=pl.BlockSpec((1, tq, GW), lambda i, j: (i, j, 0)),
        out_shape=jax.ShapeDtypeStruct((b, t, GW), BF16),
        scratch_shapes=[pltpu.VMEM((1, N_HEADS * tq), F32), pltpu.VMEM((1, N_HEADS * tq), F32),
                        pltpu.VMEM((GW, tq), F32), pltpu.VMEM((tq, N_HEADS * tq), F32)],
        compiler_params=_cparams(("arbitrary", "arbitrary")),
        name="attn_prompt",
    )(qcat, kcat, vt)


def _premix_sample_kernel(x_ref, mod_ref, win_ref, sguc_ref, sgub_ref, vec_ref, wgate_ref, bgate_ref,
                          wq_ref, wuk_ref, wpool_ref, cos_ref, sin_ref, cbuf_ref, h0_ref, pbuf_ref,
                          abd_ref, qlat_ref, qrope_ref, ckvf_ref, ckvb_ref, krf_ref, krb_ref, vrows_ref,
                          nconv_ref, nh_ref, npool_ref, *, steps, start):
    def vec(i):
        return vec_ref[i:i + 1, :]

    sh1 = mod_ref[:, 0:D_MODEL]
    sc1 = mod_ref[:, D_MODEL:2 * D_MODEL]
    win = win_ref[...]
    projs = [_dot((_ln(x_ref[t]) * (1.0 + sc1) + sh1).astype(BF16), win) for t in range(steps)]

    vgs = []
    for t in range(steps):
        vg = _ln(_gelu(projs[t][:, OFF_V:OFF_V + GW])) * vec(V_SGU_G) + vec(V_SGU_B)
        vrows_ref[t] = vg
        vgs.append(vg)
    for t in range(steps):
        mixed = sgub_ref[t:t + 1, :]
        for s in range(t + 1):
            mixed = mixed + sguc_ref[t, s:s + 1, :] * vgs[s]
        abd_ref[t, :, 0:GW] = (_gelu(projs[t][:, OFF_U:OFF_U + GW]) * mixed).astype(BF16)

    z = [cbuf_ref[k] for k in range(CONV_WIDTH - 1)] + [p[:, OFF_XB:OFF_XB + GW] for p in projs]
    hstate = h0_ref[...]
    for t in range(steps):
        xc = vec(V_CONV_B) + z[t] * vec(V_CONV_W)
        for k in range(1, CONV_WIDTH):
            xc = xc + z[t + k] * vec(V_CONV_W + k)
        a, bt = _lru_gates(xc, wgate_ref[...], bgate_ref[...], vec(V_LAM))
        hstate = a * hstate + bt
        abd_ref[t, :, GW:2 * GW] = (hstate * _gelu(projs[t][:, OFF_GB:OFF_GB + GW])).astype(BF16)
    nh_ref[...] = hstate
    for k in range(CONV_WIDTH - 1):
        nconv_ref[k] = z[steps + k]

    for t in range(steps):
        cos = cos_ref[t:t + 1, :]
        sin = sin_ref[t:t + 1, :]
        p = projs[t]
        qlat, qr = _mla_queries(p[:, OFF_CQ:OFF_CQ + GW], vec(V_QNORM), wq_ref[...], wuk_ref[...], cos, sin)
        qlat_ref[t] = qlat.astype(BF16)
        qrope_ref[t] = qr.astype(BF16)
        ckv = _rms(p[:, OFF_CKV:OFF_CKV + GW]) * vec(V_KVNORM)
        ckvf_ref[t] = ckv
        ckvb_ref[t] = ckv.astype(BF16)
        kr = (p[:, OFF_KRA:OFF_KRA + ROPE_LANES] * cos + p[:, OFF_KRB:OFF_KRB + ROPE_LANES] * sin)[:, 0:QK_ROPE]
        krf_ref[t] = kr
        krb_ref[t] = kr.astype(BF16)

    zp = [pbuf_ref[k] for k in range(POOL_BUF)] + [p[:, OFF_PD:OFF_PD + GW] for p in projs]
    wins = _pool_windows()
    for t in range(steps):
        sums = []
        run = zp[POOL_BUF + t]
        for j in range(1, POOL_WINDOWS[-1]):
            if j in POOL_WINDOWS:
                sums.append(run)
            run = run + zp[POOL_BUF + t - j]
        sums.append(run)
        cnt = jnp.minimum(float(start + t + 1), wins)
        d = (_pool_select(*sums) / cnt - zp[POOL_BUF + t]).astype(BF16)
        abd_ref[t, :, 2 * GW:3 * GW] = (_dot(d, wpool_ref[...]) * vec(V_PSCALE)).astype(BF16)
    for k in range(POOL_BUF):
        npool_ref[k] = zp[steps + k]


def _premix_sample(x_tm, mod, lw, sgu_coef, sgu_bias, cos, sin, cbuf, h0, pbuf, start):
    steps, bd, d = x_tm.shape
    assert steps <= min(CHUNK, NEW_KEY_PAD)
    f = lambda w, dt: jax.ShapeDtypeStruct((steps, bd, w), dt)
    out_shape = [f(3 * GW, BF16), f(N_HEADS * GW, BF16), f(ROPE_LANES, BF16), f(GW, F32), f(GW, BF16),
                 f(QK_ROPE, F32), f(QK_ROPE, BF16), f(GW, F32),
                 jax.ShapeDtypeStruct((CONV_WIDTH - 1, bd, GW), F32),
                 jax.ShapeDtypeStruct((bd, GW), F32),
                 jax.ShapeDtypeStruct((POOL_BUF, bd, GW), F32)]
    args = (x_tm, mod, lw["w_in"], sgu_coef, sgu_bias, lw["vec"], lw["w_gate"], lw["b_gate"],
            lw["w_q"], lw["w_uk"], lw["w_pool"], cos, sin, cbuf, h0, pbuf)
    in_specs = [_full(a.shape) for a in args]
    in_specs[1] = pl.BlockSpec((bd, 2 * d), lambda i: (0, 0))
    return pl.pallas_call(
        functools.partial(_premix_sample_kernel, steps=steps, start=start),
        grid=(1,), in_specs=in_specs,
        out_specs=[_full(s.shape) for s in out_shape], out_shape=out_shape,
        compiler_params=_cparams(("arbitrary",)),
        name="premix_sample",
    )(*args)


def _decode_kernel(pt_ref, ql_ref, qr_ref, nk_ref, nkr_ref, ckv_hbm, krt_hbm, o_ref,
                   m_ref, l_ref, acc_ref, kb_ref, sb_ref, kin_ref, krin_ref, sem, *, layer, pages, steps):
    b = pl.program_id(0)
    j = pl.program_id(1)
    nj = pl.num_programs(1)
    rows = N_HEADS * steps
    slot = j & 1
    g = b * nj + j
    land = g & 1

    def page_copies(seq, grp, buf):
        out = []
        for i in range(pages):
            page = pt_ref[seq, grp * pages + i]
            out.append(pltpu.make_async_copy(ckv_hbm.at[layer, page], kin_ref.at[buf, i], sem.at[buf]))
            out.append(pltpu.make_async_copy(krt_hbm.at[layer, page], krin_ref.at[buf, i], sem.at[buf]))
        return out

    @pl.when(g == 0)
    def _():
        for c in page_copies(0, 0, 0):
            c.start()

    @pl.when(g + 1 < pl.num_programs(0) * nj)
    def _():
        wrap = j + 1 == nj
        for c in page_copies(jnp.where(wrap, b + 1, b), jnp.where(wrap, 0, j + 1), 1 - land):
            c.start()

    for c in page_copies(b, j, land):
        c.wait()

    @pl.when(j == 0)
    def _():
        m_ref[...] = jnp.full((rows, 1), MASK_FLOOR, F32)
        l_ref[...] = jnp.zeros((rows, 1), F32)
        acc_ref[...] = jnp.zeros((rows, GW), F32)
        sb_ref[...] = jnp.full(sb_ref.shape, -jnp.inf, F32)
        kb_ref[1] = jnp.zeros(kb_ref.shape[1:], BF16)

    ql = ql_ref[0]
    qr = qr_ref[0]

    def update(s, values):
        m_old = m_ref[...]
        m_new = jnp.maximum(m_old, jnp.max(s, -1, keepdims=True))
        alpha = jnp.exp(m_old - m_new)
        p = jnp.exp(s - m_new)
        l_ref[...] = alpha * l_ref[...] + jnp.sum(p, -1, keepdims=True)
        acc_ref[...] = alpha * acc_ref[...] + _dot(p.astype(BF16), values)
        m_ref[...] = m_new

    parts = []
    for i in range(pages):
        k = kin_ref[land, i].astype(BF16)
        kb_ref[slot, i * PAGE_SIZE:(i + 1) * PAGE_SIZE, :] = k
        parts.append(_dot_nt(ql, k) + _dot(qr, krin_ref[land, i].astype(BF16)))
    s_now = jnp.concatenate(parts, -1)
    update(sb_ref[...], kb_ref[1 - slot])
    sb_ref[...] = s_now

    @pl.when(j == pl.num_programs(1) - 1)
    def _():
        update(sb_ref[...], kb_ref[slot])
        nk = nk_ref[0]
        s_new = _dot_nt(ql, nk) + _dot_nt(qr, nkr_ref[0])
        qstep = _mod(lax.broadcasted_iota(jnp.int32, (rows, NEW_KEY_PAD), 0), steps)
        kstep = lax.broadcasted_iota(jnp.int32, (rows, NEW_KEY_PAD), 1)
        update(jnp.where(kstep <= qstep, s_new, -jnp.inf), nk)
        o_ref[0] = acc_ref[...] / l_ref[...]


def _decode_attn(layer, page_table, cache_ckv, cache_krt, qlat, qrope, nk, nkr):
    bd, rows, _ = qlat.shape
    n_pages = page_table.shape[1]
    pages = min(DECODE_PAGES, n_pages)
    assert n_pages % pages == 0
    steps = rows // N_HEADS

    per_seq = lambda r, w: pl.BlockSpec((1, r, w), lambda i, j, pt: (i, 0, 0))
    in_specs = [per_seq(rows, GW), per_seq(rows, QK_ROPE), per_seq(NEW_KEY_PAD, GW), per_seq(NEW_KEY_PAD, QK_ROPE),
                pl.BlockSpec(memory_space=pl.ANY), pl.BlockSpec(memory_space=pl.ANY)]
    grid_spec = pltpu.PrefetchScalarGridSpec(
        num_scalar_prefetch=1, grid=(bd, n_pages // pages),
        in_specs=in_specs, out_specs=per_seq(rows, GW),
        scratch_shapes=[pltpu.VMEM((rows, 1), F32), pltpu.VMEM((rows, 1), F32), pltpu.VMEM((rows, GW), F32),
                        pltpu.VMEM((2, pages * PAGE_SIZE, GW), BF16),
                        pltpu.VMEM((rows, pages * PAGE_SIZE), F32),
                        pltpu.VMEM((2, pages, PAGE_SIZE, GW), F32),
                        pltpu.VMEM((2, pages, QK_ROPE, PAGE_SIZE), F32),
                        pltpu.SemaphoreType.DMA((2,))])
    return pl.pallas_call(
        functools.partial(_decode_kernel, layer=layer, pages=pages, steps=steps),
        grid_spec=grid_spec,
        out_shape=jax.ShapeDtypeStruct((bd, rows, GW), F32),
        compiler_params=_cparams(("arbitrary", "arbitrary")),
        name="decode_attn",
    )(page_table, qlat, qrope, nk, nkr, cache_ckv, cache_krt)


def _uv_kernel(o_ref, wuv_ref, out_ref):
    out = _dot(o_ref[0].astype(BF16), wuv_ref[0])
    for hh in range(1, N_HEADS):
        out = out + _dot(o_ref[hh].astype(BF16), wuv_ref[hh])
    out_ref[...] = out.astype(BF16)


def _uv_project(o_heads, wuv):
    _, n, _ = o_heads.shape
    return pl.pallas_call(
        _uv_kernel, grid=(1,),
        in_specs=[_full(o_heads.shape), _full(wuv.shape)],
        out_specs=_full((n, GW)), out_shape=jax.ShapeDtypeStruct((n, GW), BF16),
        compiler_params=_cparams(("arbitrary",)), name="uv_project",
    )(o_heads, wuv)


def _mix_norm(x_ref, abd_ref, c_ref, moda_ref, modb_ref, wabd_ref, wc_ref, lnv_ref):
    mix = _dot(abd_ref[...], wabd_ref[...]) + _dot(c_ref[...], wc_ref[...])
    g1 = moda_ref[0, :, 0:D_MODEL]
    sh2 = moda_ref[0, :, D_MODEL:2 * D_MODEL]
    sc2 = modb_ref[0, :, 0:D_MODEL]
    x1 = _ln(ALPHA * x_ref[...] + g1 * mix) * lnv_ref[0:1, :] + lnv_ref[1:2, :]
    h2 = (_ln(x1) * (1.0 + sc2) + sh2).astype(BF16)
    return x1, h2


def _swiglu(h2, w1, w3, w2):
    return _dot((_silu(_dot(h2, w1)) * _dot(h2, w3)).astype(BF16), w2)


def _post_dense_kernel(x_ref, abd_ref, c_ref, moda_ref, modb_ref, wabd_ref, wc_ref, lnv_ref,
                       w1_ref, w3_ref, w2_ref, o_ref):
    x1, h2 = _mix_norm(x_ref, abd_ref, c_ref, moda_ref, modb_ref, wabd_ref, wc_ref, lnv_ref)
    f = _swiglu(h2, w1_ref[...], w3_ref[...], w2_ref[...])
    g2 = modb_ref[0, :, D_MODEL:2 * D_MODEL]
    o_ref[...] = _ln(ALPHA * x1 + g2 * f) * lnv_ref[2:3, :] + lnv_ref[3:4, :]


def _token_specs(tm, tiles_per_group, mod_rows):
    mi = lambda blk: (lambda i, *_: (i // tiles_per_group, 0, blk))
    tok = lambda w: pl.BlockSpec((tm, w), lambda i, *_: (i, 0))
    mod = lambda blk: pl.BlockSpec((1, mod_rows, 2 * D_MODEL), mi(blk))
    return tok, mod


def _post_dense(x, abd, c, mod, lw, fw, tm, tiles_per_group):
    n, d = x.shape
    tok, modspec = _token_specs(tm, tiles_per_group, mod.shape[1])
    weights = (lw["w_abd"], lw["w_c"], lw["lnv"], fw["w1"], fw["w3"], fw["w2"])
    return pl.pallas_call(
        _post_dense_kernel, grid=(n // tm,),
        in_specs=[tok(d), tok(3 * GW), tok(GW), modspec(1), modspec(2)] + [_full(w.shape) for w in weights],
        out_specs=tok(d), out_shape=jax.ShapeDtypeStruct((n, d), F32),
        compiler_params=_cparams(("arbitrary",)), name="post_dense",
    )(x, abd, c, mod, mod, *weights)


def _post_route_kernel(x_ref, abd_ref, c_ref, moda_ref, modb_ref, wabd_ref, wc_ref, lnv_ref,
                       wr_ref, br_ref, x1_ref, h2_ref, route_ref, cnt_ref, run_ref):
    tm = x_ref.shape[0]

    @pl.when(pl.program_id(0) == 0)
    def _():
        run_ref[...] = jnp.zeros((1, 128), F32)

    x1, h2 = _mix_norm(x_ref, abd_ref, c_ref, moda_ref, modb_ref, wabd_ref, wc_ref, lnv_ref)
    x1_ref[...] = x1
    h2_ref[...] = h2
    lane = lax.broadcasted_iota(jnp.int32, (1, 128), 1)
    logits = jnp.where(lane < N_EXPERTS, _dot(h2, wr_ref[...]) + br_ref[...], -jnp.inf)
    m1 = jnp.max(logits, -1, keepdims=True)
    i1 = jnp.min(jnp.where(logits == m1, lane, 128), -1, keepdims=True)
    rest = jnp.where(lane == i1, -jnp.inf, logits)
    m2 = jnp.max(rest, -1, keepdims=True)
    i2 = jnp.min(jnp.where(rest == m2, lane, 128), -1, keepdims=True)
    e2 = jnp.exp(m2 - m1)
    gate1 = 1.0 / (1.0 + e2)
    gate2 = e2 / (1.0 + e2)
    sel = jnp.where((lane == i1) | (lane == i2), 1.0, 0.0)
    r = lax.broadcasted_iota(jnp.int32, (tm, tm), 0)
    c = lax.broadcasted_iota(jnp.int32, (tm, tm), 1)
    before = _dot(jnp.where(c < r, 1.0, 0.0).astype(BF16), sel.astype(BF16)) + run_ref[...]
    rank1 = jnp.sum(jnp.where(lane == i1, before, 0.0), -1, keepdims=True)
    rank2 = jnp.sum(jnp.where(lane == i2, before, 0.0), -1, keepdims=True)
    run_ref[...] = run_ref[...] + jnp.sum(sel, 0, keepdims=True)
    cnt_ref[...] = run_ref[...]
    cols = (i1.astype(F32), i2.astype(F32), gate1, gate2, rank1, rank2)
    route = jnp.zeros((tm, 128), F32)
    for k, v in enumerate(cols):
        route = jnp.where(lane == k, v, route)
    route_ref[...] = route


def _post_route(x, abd, c, mod, lw, mw, tm, tiles_per_group):
    n, d = x.shape
    tok, modspec = _token_specs(tm, tiles_per_group, mod.shape[1])
    weights = (lw["w_abd"], lw["w_c"], lw["lnv"], mw["w_router"], mw["b_router"])
    return pl.pallas_call(
        _post_route_kernel, grid=(n // tm,),
        in_specs=[tok(d), tok(3 * GW), tok(GW), modspec(1), modspec(2)] + [_full(w.shape) for w in weights],
        out_specs=[tok(d), tok(d), tok(128), _full((1, 128))],
        out_shape=[jax.ShapeDtypeStruct((n, d), F32), jax.ShapeDtypeStruct((n, d), BF16),
                   jax.ShapeDtypeStruct((n, 128), F32), jax.ShapeDtypeStruct((1, 128), F32)],
        scratch_shapes=[pltpu.VMEM((1, 128), F32)],
        compiler_params=_cparams(("arbitrary",)), name="post_route",
    )(x, abd, c, mod, mod, *weights)


def _moe_group_kernel(tile_ref, blk_ref, first_ref, last_ref, valid_ref, texp_ref,
                      h2_ref, post_ref, w1_ref, w3_ref, w2_ref, out_ref, xg_ref):
    del blk_ref, texp_ref
    w = pl.program_id(0)

    @pl.when(first_ref[w] == 1)
    def _():
        xg_ref[...] = jnp.zeros_like(xg_ref)

    @pl.when(valid_ref[w] == 1)
    def _():
        row = tile_ref[w] * MOE_TILE + lax.broadcasted_iota(jnp.int32, (MOE_TILE, 1), 0)
        pick = (post_ref[0:1, :] == row) | (post_ref[1:2, :] == row)
        xg_ref[...] += _dot(jnp.where(pick, 1.0, 0.0).astype(BF16), h2_ref[...])

    @pl.when(last_ref[w] == 1)
    def _():
        out_ref[...] = _swiglu(xg_ref[...].astype(BF16), w1_ref[...], w3_ref[...], w2_ref[...]).astype(BF16)

    @pl.when(valid_ref[w] == 0)
    def _():
        out_ref[...] = jnp.zeros_like(out_ref)


def _moe_group(items, item_expert, h2, pos_t, mw, n_rows):
    n, d = h2.shape
    n_items = items[0].shape[0]
    wspec = lambda shape: pl.BlockSpec((None,) + shape, lambda w, tile, blk, f, l, v, te: (te[w], 0, 0))
    grid_spec = pltpu.PrefetchScalarGridSpec(
        num_scalar_prefetch=6, grid=(n_items,),
        in_specs=[pl.BlockSpec((MOE_BLOCK, d), lambda w, tile, blk, f, l, v, te: (blk[w], 0)),
                  pl.BlockSpec((2, MOE_BLOCK), lambda w, tile, blk, f, l, v, te: (0, blk[w])),
                  wspec((d, D_FF)), wspec((d, D_FF)), wspec((D_FF, d))],
        out_specs=pl.BlockSpec((MOE_TILE, d), lambda w, tile, blk, f, l, v, te: (tile[w], 0)),
        scratch_shapes=[pltpu.VMEM((MOE_TILE, d), F32)])
    return pl.pallas_call(
        _moe_group_kernel, grid_spec=grid_spec,
        out_shape=jax.ShapeDtypeStruct((n_rows, d), BF16),
        compiler_params=_cparams(("arbitrary",)), name="moe_group",
    )(*items, item_expert, h2, pos_t, mw["w1"], mw["w3"], mw["w2"])


def _moe_combine_kernel(blk_ref, tile_ref, first_ref, last_ref, valid_ref,
                        out_ref, pos_ref, gate_ref, y_ref, acc_ref):
    del blk_ref
    w = pl.program_id(0)

    @pl.when(first_ref[w] == 1)
    def _():
        acc_ref[...] = jnp.zeros_like(acc_ref)

    @pl.when(valid_ref[w] == 1)
    def _():
        row = tile_ref[w] * MOE_TILE + lax.broadcasted_iota(jnp.int32, (1, MOE_TILE), 1)
        sel = (jnp.where(pos_ref[:, 0:1] == row, gate_ref[:, 0:1], 0.0)
               + jnp.where(pos_ref[:, 1:2] == row, gate_ref[:, 1:2], 0.0))
        acc_ref[...] += _dot(sel.astype(BF16), out_ref[...])

    @pl.when(last_ref[w] == 1)
    def _():
        y_ref[...] = acc_ref[...]


def _moe_combine(items, out_sorted, pos, gates):
    n = pos.shape[0]
    d = out_sorted.shape[1]
    n_items = items[0].shape[0]
    tokb = lambda w_: pl.BlockSpec((MOE_BLOCK, w_), lambda w, blk, tile, f, l, v: (blk[w], 0))
    grid_spec = pltpu.PrefetchScalarGridSpec(
        num_scalar_prefetch=5, grid=(n_items,),
        in_specs=[pl.BlockSpec((MOE_TILE, d), lambda w, blk, tile, f, l, v: (tile[w], 0)), tokb(2), tokb(2)],
        out_specs=tokb(d),
        scratch_shapes=[pltpu.VMEM((MOE_BLOCK, d), F32)])
    return pl.pallas_call(
        _moe_combine_kernel, grid_spec=grid_spec,
        out_shape=jax.ShapeDtypeStruct((n, d), F32),
        compiler_params=_cparams(("arbitrary",)), name="moe_combine",
    )(*items, out_sorted, pos, gates)


def _moe_final_kernel(x1_ref, y_ref, modb_ref, lnv_ref, o_ref):
    g2 = modb_ref[0, :, D_MODEL:2 * D_MODEL]
    o_ref[...] = _ln(ALPHA * x1_ref[...] + g2 * y_ref[...]) * lnv_ref[2:3, :] + lnv_ref[3:4, :]


def _moe_final(x1, y, y_row0, mod, lw, tm, tiles_per_group):
    n, d = x1.shape
    assert y_row0 % tm == 0
    tok, modspec = _token_specs(tm, tiles_per_group, mod.shape[1])
    return pl.pallas_call(
        _moe_final_kernel, grid=(n // tm,),
        in_specs=[tok(d), pl.BlockSpec((tm, d), lambda i: (i + y_row0 // tm, 0)), modspec(2),
                  _full(lw["lnv"].shape)],
        out_specs=tok(d), out_shape=jax.ShapeDtypeStruct((n, d), F32),
        compiler_params=_cparams(("arbitrary",)), name="moe_final",
    )(x1, y, mod, lw["lnv"])


def _pair_items(incidence, n_items):
    n_cols = incidence.shape[1]
    running = jnp.cumsum(incidence.reshape(-1).astype(jnp.int32))
    n_valid = running[-1]
    idx = jnp.arange(n_items)
    flat = jnp.minimum(jnp.searchsorted(running, idx + 1, side="left", method="compare_all"), running.shape[0] - 1)
    rows, cols = flat // n_cols, flat % n_cols
    valid = idx < n_valid
    last_valid = jnp.maximum(n_valid - 1, 0)
    rows = jnp.where(valid, rows, rows[last_valid]).astype(jnp.int32)
    cols = jnp.where(valid, cols, cols[last_valid]).astype(jnp.int32)
    prev_rows = jnp.concatenate([rows[:1] - 1, rows[:-1]])
    next_rows = jnp.concatenate([rows[1:], rows[-1:] + 1])
    first = valid & (rows != prev_rows)
    last = valid & ((rows != next_rows) | (idx == last_valid))
    return rows, cols, first.astype(jnp.int32), last.astype(jnp.int32), valid.astype(jnp.int32)


def _moe_ffn(h2, route, counts, mw):
    n, d = h2.shape
    assert n % MOE_BLOCK == 0
    n_blocks = n // MOE_BLOCK
    n_tiles = (2 * n + N_EXPERTS * (MOE_TILE - 1)) // MOE_TILE + 1
    n_items = n_tiles + N_EXPERTS * n_blocks
    experts = route[:, 0:2].astype(jnp.int32)
    gates = route[:, 2:4]
    tiles_per_expert = (counts.astype(jnp.int32) + MOE_TILE - 1) // MOE_TILE
    tile_end = jnp.cumsum(tiles_per_expert)
    row_start = (tile_end - tiles_per_expert) * MOE_TILE
    pos = row_start[experts] + route[:, 4:6].astype(jnp.int32)
    tile_expert = jnp.minimum(jnp.searchsorted(tile_end, jnp.arange(n_tiles), side="right"),
                              N_EXPERTS - 1).astype(jnp.int32)
    pick_tile = (pos // MOE_TILE).reshape(n_blocks, 1, 2 * MOE_BLOCK)
    incidence = jnp.any(pick_tile == jnp.arange(n_tiles)[None, :, None], -1)
    tile_i, blk_i, first_i, last_i, valid_i = _pair_items(incidence.T, n_items)
    item_expert = tile_expert[tile_i]
    n_valid = jnp.sum(valid_i)
    spare_tile = jnp.minimum(tile_end[-1] + jnp.arange(n_items) - n_valid, n_tiles - 1)
    tile_i = jnp.where(valid_i == 1, tile_i, spare_tile).astype(jnp.int32)
    out_sorted = _moe_group((tile_i, blk_i, first_i, last_i, valid_i), item_expert, h2, pos.T, mw,
                            n_tiles * MOE_TILE)
    return _moe_combine(_pair_items(incidence, n_items), out_sorted, pos, gates)


def _block_diag(blocks):
    n, r, c = blocks.shape
    on_diag = jnp.eye(n, dtype=bool)[:, None, :, None]
    return jnp.where(on_diag, blocks[:, :, None, :], 0).reshape(n * r, n * c)


def _swap_rope_halves(w):
    half = QK_ROPE // 2
    return jnp.concatenate([w[..., half:], w[..., :half]], -1)


def _layer_weights(l, p):
    w_in = p["w_in"][l]
    s = np.cumsum([GW, GW, GW, GW, GW, GW, QK_ROPE])
    kr_cols = w_in[:, s[5]:s[6]]
    gap = jnp.zeros_like(kr_cols)
    kr_sw = _swap_rope_halves(kr_cols)
    w_in_r = jnp.concatenate(
        [w_in[:, :s[5]], w_in[:, s[6]:], kr_cols, gap, kr_cols, gap, kr_sw, gap, kr_sw, gap], 1).astype(BF16)
    vec_names = ("sgu_ln_g", "sgu_ln_b", "lru_conv_b", "lru_lambda", "mla_q_norm", "mla_kv_norm", "pool_scale")
    assert (V_SGU_G, V_SGU_B, V_CONV_B, V_LAM, V_QNORM, V_KVNORM, V_PSCALE) == tuple(range(len(vec_names)))
    vec = jnp.concatenate(
        [jnp.stack([p[name][l] for name in vec_names]), jnp.zeros((V_CONV_W - len(vec_names), GW), F32),
         p["lru_conv_w"][l], jnp.zeros((VEC_ROWS - V_CONV_W - CONV_WIDTH, GW), F32)], 0)
    w_uq = p["mla_w_uq"][l]
    q_rope_cols = w_uq[:, :, QK_NOPE:]
    w_q = jnp.concatenate([w_uq[:, :, :QK_NOPE].reshape(GW, GW), q_rope_cols.reshape(GW, ROPE_LANES),
                           _swap_rope_halves(q_rope_cols).reshape(GW, ROPE_LANES)], 1).astype(BF16)
    w_uk = _block_diag(jnp.transpose(p["mla_w_uk"][l], (1, 2, 0))).astype(BF16)
    w_uv = p["mla_w_uv"][l]
    zq = jnp.zeros((GW, N_HEADS, HEAD_LANES - QK_NOPE - QK_ROPE), F32)
    w_qh = jnp.concatenate(
        [jnp.concatenate([w_uq, zq], -1).reshape(GW, N_HEADS * HEAD_LANES),
         jnp.concatenate([jnp.zeros((GW, N_HEADS, QK_NOPE), F32), _swap_rope_halves(q_rope_cols), zq],
                         -1).reshape(GW, N_HEADS * HEAD_LANES)], 1).astype(BF16)
    w_kn = jnp.concatenate([p["mla_w_uk"][l], jnp.zeros((GW, N_HEADS, HEAD_LANES - QK_NOPE), F32)], -1)
    w_kv = jnp.concatenate([w_kn.reshape(GW, N_HEADS * HEAD_LANES), w_uv.reshape(GW, GW)], 1).astype(BF16)
    w_uv_pad = jnp.where(jnp.eye(N_HEADS, dtype=bool)[:, None, :, None], w_uv[None], 0).reshape(N_HEADS, GW, GW)
    w_out = p["w_out"][l]
    return {
        "w_in": w_in_r,
        "sgu_w": p["sgu_w"][l],
        "sgu_b": p["sgu_b"][l],
        "vec": vec,
        "w_gate": jnp.concatenate([_block_diag(p["lru_wa"][l]), _block_diag(p["lru_wx"][l])], 1).astype(BF16),
        "b_gate": jnp.concatenate([p["lru_ba"][l], p["lru_bx"][l]])[None, :],
        "w_q": w_q,
        "w_uk": w_uk,
        "w_qh": w_qh,
        "w_kv": w_kv,
        "w_uv": w_uv_pad.astype(BF16),
        "w_pool": _block_diag(p["pool_w"][l]).astype(BF16),
        "w_abd": jnp.concatenate([w_out[:2 * GW], w_out[3 * GW:]], 0).astype(BF16),
        "w_c": w_out[2 * GW:3 * GW].astype(BF16),
        "lnv": jnp.stack([p["ln1_g"][l], p["ln1_b"][l], p["ln2_g"][l], p["ln2_b"][l]]),
    }


def _rope_tables(start, steps):
    half = QK_ROPE // 2
    freq = ROPE_THETA ** (-jnp.arange(half, dtype=F32) / half)
    ang = (start + jnp.arange(steps)).astype(F32)[:, None] * freq[None, :]
    cos = jnp.cos(ang)
    sin = jnp.sin(ang)
    return jnp.concatenate([cos, cos], -1), jnp.concatenate([-sin, sin], -1)


def _rope_tiled(start, steps):
    cos, sin = _rope_tables(start, steps)
    return jnp.tile(cos, (1, N_HEADS)), jnp.tile(sin, (1, N_HEADS))


def _rope_head_tables(start, steps):
    cos, sin = _rope_tables(start, steps)
    one = jnp.ones((steps, QK_NOPE), F32)
    z64 = jnp.zeros((steps, QK_NOPE), F32)
    z32 = jnp.zeros((steps, QK_ROPE), F32)
    return jnp.concatenate([one, cos, z32, z64, sin, z32, cos, z32, cos, z32, sin, z32, sin, z32], 1)


def kernel(x_prompt, x_sample, c_prompt, c_sample, cache_ckv, cache_kr, page_table, state_lru_conv, state_lru_h, state_pool, ada_w, ada_b, w_in, w_out, ln1_g, ln1_b, ln2_g, ln2_b, sgu_ln_g, sgu_ln_b, sgu_w, sgu_b, lru_conv_w, lru_conv_b, lru_wa, lru_ba, lru_wx, lru_bx, lru_lambda, mla_q_norm, mla_w_uq, mla_kv_norm, mla_w_uk, mla_w_uv, pool_w, pool_scale, ffn_w1, ffn_w3, ffn_w2, moe_router, moe_router_b, moe_w1, moe_w3, moe_w2):
    p = dict(w_in=w_in, w_out=w_out, ln1_g=ln1_g, ln1_b=ln1_b, ln2_g=ln2_g, ln2_b=ln2_b,
             sgu_ln_g=sgu_ln_g, sgu_ln_b=sgu_ln_b, sgu_w=sgu_w, sgu_b=sgu_b, lru_conv_w=lru_conv_w,
             lru_conv_b=lru_conv_b, lru_wa=lru_wa, lru_ba=lru_ba, lru_wx=lru_wx, lru_bx=lru_bx,
             lru_lambda=lru_lambda, mla_q_norm=mla_q_norm, mla_w_uq=mla_w_uq, mla_kv_norm=mla_kv_norm,
             mla_w_uk=mla_w_uk, mla_w_uv=mla_w_uv, pool_w=pool_w, pool_scale=pool_scale)
    bp, tp, d = x_prompt.shape
    bd, td, _ = x_sample.shape
    n_pages = page_table.shape[1]
    past_len = n_pages * PAGE_SIZE
    tmp = PROMPT_TILE

    rope_p = _rope_head_tables(0, tp)
    cos_s, sin_s = _rope_tiled(past_len, td)
    cache_krt = jnp.swapaxes(cache_kr, 2, 3)
    pad = (-(bd + bp)) % 8
    c_all = jnp.concatenate([c_sample, c_prompt, jnp.zeros((pad, d), F32)], 0)

    xp = x_prompt
    xs = jnp.transpose(x_sample, (1, 0, 2))
    sp = [[] for _ in range(5)]
    ss = [[] for _ in range(6)]
    for l in range(DEPTH):
        lw = _layer_weights(l, p)
        mod = _ada(c_all, ada_w[l].astype(BF16), ada_b[l][None, :])
        mod_s = mod[:bd]
        mod_p = mod[bd:bd + bp][:, None, :]

        (abd_p, qcat_p, kcat_p, vt_p, ckv_p, kr_p, nconv_p, nh_p, npool_p) = _premix_prompt(
            xp, mod_p, dict(lw, sgu_w=lw["sgu_w"][:, :CHUNK, :CHUNK],
                            sgu_bias=jnp.repeat(lw["sgu_b"][:, :CHUNK].T, HEAD_DIM, axis=1)), rope_p)
        c_p = _attn_prompt(qcat_p, kcat_p, vt_p)

        sgu_coef = jnp.repeat(jnp.transpose(lw["sgu_w"][:, :td, :td], (1, 2, 0)), HEAD_DIM, axis=2)
        sgu_bias_s = jnp.repeat(lw["sgu_b"][:, :td].T, HEAD_DIM, axis=1)
        (abd_s, qlat_s, qrope_s, ckv_s, ckvb_s, kr_s, krb_s, vrows_s, nconv_s, nh_s, npool_s) = _premix_sample(
            xs, mod_s, lw, sgu_coef, sgu_bias_s, cos_s, sin_s,
            jnp.transpose(state_lru_conv[l], (1, 0, 2)), state_lru_h[l],
            jnp.transpose(state_pool[l], (1, 0, 2)), past_len)
        to_rows = lambda a, w: jnp.transpose(a.reshape(td, bd, N_HEADS, w), (1, 2, 0, 3)).reshape(bd, N_HEADS * td, w)
        pad_keys = lambda a: jnp.pad(jnp.transpose(a, (1, 0, 2)), ((0, 0), (0, NEW_KEY_PAD - td), (0, 0)))
        o_s = _decode_attn(l, page_table, cache_ckv, cache_krt, to_rows(qlat_s, GW), to_rows(qrope_s, QK_ROPE),
                           pad_keys(ckvb_s), pad_keys(krb_s))
        o_heads = jnp.transpose(o_s.reshape(bd, N_HEADS, td, GW), (1, 2, 0, 3)).reshape(N_HEADS, td * bd, GW)
        c_s = _uv_project(o_heads, lw["w_uv"])

        xp2 = xp.reshape(bp * tp, d)
        xs2 = xs.reshape(td * bd, d)
        args_p = (xp2, abd_p.reshape(bp * tp, 3 * GW), c_p.reshape(bp * tp, GW), mod_p)
        args_s = (xs2, abd_s.reshape(td * bd, 3 * GW), c_s, mod_s[None])
        j = l // 2
        if l % 2 == 0:
            fw = {"w1": ffn_w1[j].astype(BF16), "w3": ffn_w3[j].astype(BF16), "w2": ffn_w2[j].astype(BF16)}
            xp2 = _post_dense(*args_p, lw, fw, tmp, tp // tmp)
            xs2 = _post_dense(*args_s, lw, fw, bd, td)
        else:
            mw = {"w_router": jnp.pad(moe_router[j], ((0, 0), (0, 128 - N_EXPERTS))).astype(BF16),
                  "b_router": jnp.pad(moe_router_b[j], (0, 128 - N_EXPERTS))[None, :],
                  "w1": moe_w1[j].astype(BF16), "w3": moe_w3[j].astype(BF16), "w2": moe_w2[j].astype(BF16)}
            x1p, h2p, route_p, cnt_p = _post_route(*args_p, lw, mw, tmp, tp // tmp)
            x1s, h2s, route_s, cnt_s = _post_route(*args_s, lw, mw, bd, td)
            before_s = cnt_p[0, :N_EXPERTS][route_s[:, 0:2].astype(jnp.int32)]
            route = jnp.concatenate([route_p, route_s.at[:, 4:6].add(before_s)], 0)
            y = _moe_ffn(jnp.concatenate([h2p, h2s], 0), route, (cnt_p + cnt_s)[0, :N_EXPERTS], mw)
            xp2 = _moe_final(x1p, y, 0, mod_p, lw, tmp, tp // tmp)
            xs2 = _moe_final(x1s, y, bp * tp, mod_s[None], lw, bd, td)
        xp = xp2.reshape(bp, tp, d)
        xs = xs2.reshape(td, bd, d)

        for k, v in enumerate((ckv_p, kr_p, nconv_p, nh_p[:, 0], npool_p)):
            sp[k].append(v)
        bt = lambda a: jnp.transpose(a, (1, 0, 2))
        for k, v in enumerate((bt(ckv_s), bt(kr_s), bt(nconv_s), nh_s, bt(npool_s), bt(vrows_s))):
            ss[k].append(v)

    return (xp, jnp.transpose(xs, (1, 0, 2)),
            jnp.stack(sp[0]), jnp.stack(sp[1]), jnp.stack(sp[2]), jnp.stack(sp[3]), jnp.stack(sp[4]),
            jnp.stack(ss[0]), jnp.stack(ss[1]), jnp.stack(ss[2]), jnp.stack(ss[3]), jnp.stack(ss[4]),
            jnp.stack(ss[5]))
```

```python
import functools

import jax
import jax.numpy as jnp
import numpy as np
from jax import lax
from jax.experimental import pallas as pl
from jax.experimental.pallas import tpu as pltpu

F32 = jnp.float32
BF16 = jnp.bfloat16

D_MODEL = 1024
DEPTH = 2
PAGE_SIZE = 128
GROUP_WIDTH = D_MODEL // 4
HEAD_DIM = 64
N_HEADS = GROUP_WIDTH // HEAD_DIM
CHUNK = 128
CONV_WIDTH = 4
LRU_C = 8.0
QK_NOPE = HEAD_DIM
QK_ROPE = HEAD_DIM // 2
MLA_SCALE = (QK_NOPE + QK_ROPE) ** -0.5
ROPE_THETA = 10000.0
POOL_WINDOWS = (2, 4, 8, 16)
POOL_BUF = 15
POOL_GC = GROUP_WIDTH // 4
D_FF = ((8 * D_MODEL // 3 + 127) // 128) * 128
N_EXPERTS = 8
ALPHA = (2 * DEPTH) ** 0.25
LN_EPS = 1e-5
RMS_EPS = 1e-6

GW = GROUP_WIDTH
OFF_U, OFF_V, OFF_XB, OFF_GB, OFF_CQ, OFF_CKV, OFF_PD, OFF_KRA, OFF_KRB = (
    0, GW, 2 * GW, 3 * GW, 4 * GW, 5 * GW, 6 * GW, 7 * GW, 7 * GW + 128)
P_IN_PAD = 8 * GW
ROPE_LANES = N_HEADS * QK_ROPE
HEAD_LANES = 128

V_SGU_G, V_SGU_B, V_CONV_B, V_LAM, V_QNORM, V_KVNORM, V_PSCALE = 0, 1, 2, 3, 4, 5, 6
V_CONV_W = 8
VEC_ROWS = 16

VMEM_LIMIT_BYTES = 56 * 1024 * 1024
PROMPT_TILE = 256
ATTN_TILE = 256
DECODE_PAGES = 16
NEW_KEY_PAD = 16
MASK_FLOOR = -3.0e38
MOE_TILE = 256
MOE_BLOCK = 512


def _dot(a, b):
    return jnp.dot(a, b, preferred_element_type=F32)


def _dot_nt(a, b):
    return lax.dot_general(a, b, (((1,), (1,)), ((), ())), preferred_element_type=F32)


def _ln(x):
    mu = jnp.mean(x, -1, keepdims=True)
    xc = x - mu
    var = jnp.mean(xc * xc, -1, keepdims=True)
    return xc * lax.rsqrt(var + LN_EPS)


def _rms(x):
    return x * lax.rsqrt(jnp.mean(x * x, -1, keepdims=True) + RMS_EPS)


def _gelu(x):
    return 0.5 * x * (1.0 + jnp.tanh(0.7978845608028654 * (x + 0.044715 * (x * x * x))))


def _silu(x):
    return x * jax.nn.sigmoid(x)


def _softplus(x):
    return jnp.maximum(x, 0.0) + jnp.log1p(jnp.exp(-jnp.abs(x)))


def _lane_group(width, group):
    assert group & (group - 1) == 0
    return lax.broadcasted_iota(jnp.int32, (1, width), 1) >> (group.bit_length() - 1)


def _mod(x, n):
    return x & (n - 1) if n & (n - 1) == 0 else lax.rem(x, n)


def _cparams(sem):
    return pltpu.CompilerParams(dimension_semantics=sem, vmem_limit_bytes=VMEM_LIMIT_BYTES)


def _full(shape):
    n = len(shape)
    return pl.BlockSpec(shape, lambda *_: (0,) * n)


def _ada_kernel(c_ref, w_ref, b_ref, o_ref):
    o_ref[...] = _dot(_silu(c_ref[...]).astype(BF16), w_ref[...]) + b_ref[...]


def _ada(c_all, w, b):
    m, d = c_all.shape
    n = w.shape[1]
    tn = 512
    return pl.pallas_call(
        _ada_kernel,
        grid=(n // tn,),
        in_specs=[pl.BlockSpec((m, d), lambda j: (0, 0)),
                  pl.BlockSpec((d, tn), lambda j: (0, j)),
                  pl.BlockSpec((1, tn), lambda j: (0, j))],
        out_specs=pl.BlockSpec((m, tn), lambda j: (0, j)),
        out_shape=jax.ShapeDtypeStruct((m, n), F32),
        compiler_params=_cparams(("arbitrary",)),
        name="ada",
    )(c_all, w, b)


def _lru_gates(xc, wgate, bgate, lam):
    g = _dot(xc.astype(BF16), wgate) + bgate
    r = jax.nn.sigmoid(g[:, :GW])
    i = jax.nn.sigmoid(g[:, GW:])
    log_a = (-LRU_C) * r * _softplus(-lam)
    a = jnp.exp(log_a)
    bt = jnp.sqrt(-jnp.tanh(log_a) * (a * a + 1.0)) * i * xc
    return a, bt


def _mla_queries(cq, qnorm, wq, wuk, cos, sin):
    q = _dot((_rms(cq) * qnorm).astype(BF16), wq)
    qlat = _dot(q[:, :GW].astype(BF16), wuk) * MLA_SCALE
    qr = (q[:, GW:GW + ROPE_LANES] * cos + q[:, GW + ROPE_LANES:] * sin) * MLA_SCALE
    return qlat, qr


def _pool_select(s2, s4, s8, s16):
    grp = _lane_group(GW, POOL_GC)
    return jnp.where(grp == 0, s2, jnp.where(grp == 1, s4, jnp.where(grp == 2, s8, s16)))


def _pool_windows():
    grp = _lane_group(GW, POOL_GC)
    w = jnp.where(grp == 0, POOL_WINDOWS[0],
                  jnp.where(grp == 1, POOL_WINDOWS[1],
                            jnp.where(grp == 2, POOL_WINDOWS[2], POOL_WINDOWS[3])))
    return w.astype(F32)


def _premix_prompt_kernel(x_ref, mod_ref, win_ref, sguw_ref, sgub_ref, vec_ref, wgate_ref, bgate_ref,
                          wq_ref, wkv_ref, wpool_ref, rope_ref,
                          abd_ref, qcat_ref, kcat_ref, vt_ref, ckvf_ref, krf_ref,
                          nconv_ref, nh_ref, npool_ref,
                          zc_ref, zp_ref, hc_ref, *, tm):
    t = pl.program_id(1)

    @pl.when(t == 0)
    def _():
        zc_ref[0:8, :] = jnp.zeros((8, GW), F32)
        zp_ref[0:16, :] = jnp.zeros((16, GW), F32)
        hc_ref[...] = jnp.zeros((1, GW), F32)

    def vec(i):
        return vec_ref[i:i + 1, :]

    x = x_ref[0]
    sh1 = mod_ref[0, :, 0:D_MODEL]
    sc1 = mod_ref[0, :, D_MODEL:2 * D_MODEL]
    h = _ln(x) * (1.0 + sc1) + sh1
    proj = _dot(h.astype(BF16), win_ref[...])

    ug = _gelu(proj[:, OFF_U:OFF_U + GW])
    vg = _ln(_gelu(proj[:, OFF_V:OFF_V + GW])) * vec(V_SGU_G) + vec(V_SGU_B)
    vgb = vg.astype(BF16)
    head = _lane_group(GW, HEAD_DIM)
    row = lax.broadcasted_iota(jnp.int32, (CHUNK, CHUNK), 0)
    col = lax.broadcasted_iota(jnp.int32, (CHUNK, CHUNK), 1)
    wmix = [jnp.where(col <= row, sguw_ref[hh], 0.0).astype(BF16) for hh in range(N_HEADS)]
    parts = []
    for c in range(tm // CHUNK):
        vc = vgb[c * CHUNK:(c + 1) * CHUNK]
        acc = _dot(wmix[0], vc)
        for hh in range(1, N_HEADS):
            acc = jnp.where(head == hh, _dot(wmix[hh], vc), acc)
        parts.append(acc + sgub_ref[...])
    mixed = jnp.concatenate(parts, 0) if len(parts) > 1 else parts[0]
    abd_ref[0, :, 0:GW] = (ug * mixed).astype(BF16)

    xb = proj[:, OFF_XB:OFF_XB + GW]
    zc_ref[8:8 + tm, :] = xb
    z = zc_ref[...]
    xc = (vec(V_CONV_B) + vec(V_CONV_W + 3) * z + vec(V_CONV_W + 2) * pltpu.roll(z, 1, 0)
          + vec(V_CONV_W + 1) * pltpu.roll(z, 2, 0) + vec(V_CONV_W) * pltpu.roll(z, 3, 0))[8:]
    nconv_ref[0] = zc_ref[pl.ds(8 + tm - (CONV_WIDTH - 1), CONV_WIDTH - 1), :]
    zc_ref[0:8, :] = xb[tm - 8:tm]
    a, bt = _lru_gates(xc, wgate_ref[...], bgate_ref[...], vec(V_LAM))
    rowi = lax.broadcasted_iota(jnp.int32, (tm, GW), 0)
    k = 1
    while k < tm:
        keep = rowi >= k
        bt = jnp.where(keep, a * pltpu.roll(bt, k, 0) + bt, bt)
        a = jnp.where(keep, a * pltpu.roll(a, k, 0), a)
        k *= 2
    hs = a * hc_ref[...] + bt
    hc_ref[...] = hs[tm - 1:tm]
    nh_ref[0] = hs[tm - 1:tm]
    abd_ref[0, :, GW:2 * GW] = (hs * _gelu(proj[:, OFF_GB:OFF_GB + GW])).astype(BF16)

    cosq = rope_ref[:, 0:HEAD_LANES]
    sinq = rope_ref[:, HEAD_LANES:2 * HEAD_LANES]
    cosk = rope_ref[:, 2 * HEAD_LANES:3 * HEAD_LANES]
    sink = rope_ref[:, 3 * HEAD_LANES:4 * HEAD_LANES]
    q = _dot((_rms(proj[:, OFF_CQ:OFF_CQ + GW]) * vec(V_QNORM)).astype(BF16), wq_ref[...])
    half = N_HEADS * HEAD_LANES
    for hh in range(N_HEADS):
        lo = hh * HEAD_LANES
        qh = q[:, lo:lo + HEAD_LANES] * cosq + q[:, half + lo:half + lo + HEAD_LANES] * sinq
        qcat_ref[0, hh] = (qh * MLA_SCALE).astype(BF16)
    ckv = _rms(proj[:, OFF_CKV:OFF_CKV + GW]) * vec(V_KVNORM)
    ckvf_ref[0] = ckv
    kv = _dot(ckv.astype(BF16), wkv_ref[...])
    kr = proj[:, OFF_KRA:OFF_KRA + HEAD_LANES] * cosk + proj[:, OFF_KRB:OFF_KRB + HEAD_LANES] * sink
    krf_ref[0] = kr[:, 0:QK_ROPE]
    kr_hi = jnp.where(lax.broadcasted_iota(jnp.int32, (1, HEAD_LANES), 1) >= QK_NOPE, kr, 0.0)
    for hh in range(N_HEADS):
        kcat_ref[0, hh] = (kv[:, hh * HEAD_LANES:(hh + 1) * HEAD_LANES] + kr_hi).astype(BF16)
    vt_ref[0, 0] = kv[:, half:half + GW].T.astype(BF16)

    pd = proj[:, OFF_PD:OFF_PD + GW]
    zp_ref[16:16 + tm, :] = pd
    zz = zp_ref[...]
    s2 = zz + pltpu.roll(zz, 1, 0)
    s4 = s2 + pltpu.roll(s2, 2, 0)
    s8 = s4 + pltpu.roll(s4, 4, 0)
    s16 = s8 + pltpu.roll(s8, 8, 0)
    wsum = _pool_select(s2, s4, s8, s16)[16:]
    pos = (t * tm + lax.broadcasted_iota(jnp.int32, (tm, 1), 0) + 1).astype(F32)
    cnt = jnp.minimum(pos, _pool_windows())
    d = (wsum / cnt - pd).astype(BF16)
    abd_ref[0, :, 2 * GW:3 * GW] = (_dot(d, wpool_ref[...]) * vec(V_PSCALE)).astype(BF16)
    npool_ref[0] = zp_ref[pl.ds(16 + tm - POOL_BUF, POOL_BUF), :]
    zp_ref[0:16, :] = pd[tm - 16:tm]


def _premix_prompt(x, mod, lw, rope):
    b, t, d = x.shape
    tm = PROMPT_TILE
    assert t % tm == 0 and tm % CHUNK == 0 and tm == ATTN_TILE
    grid = (b, t // tm)
    tok = lambda w: pl.BlockSpec((1, tm, w), lambda i, j: (i, j, 0))
    per_b = lambda r: pl.BlockSpec((1, r, GW), lambda i, j: (i, 0, 0))
    heads = pl.BlockSpec((1, N_HEADS, tm, HEAD_LANES), lambda i, j: (i, 0, j, 0))
    in_specs = [
        tok(d),
        pl.BlockSpec((1, 1, 2 * d), lambda i, j: (i, 0, 0)),
        _full(lw["w_in"].shape), _full(lw["sgu_w"].shape), _full(lw["sgu_bias"].shape),
        _full(lw["vec"].shape), _full(lw["w_gate"].shape), _full(lw["b_gate"].shape),
        _full(lw["w_qh"].shape), _full(lw["w_kv"].shape), _full(lw["w_pool"].shape),
        pl.BlockSpec((tm, 4 * HEAD_LANES), lambda i, j: (j, 0)),
    ]
    out_specs = [
        tok(3 * GW), heads, heads,
        pl.BlockSpec((1, 1, GW, tm), lambda i, j: (i, j, 0, 0)),
        tok(GW), tok(QK_ROPE),
        per_b(CONV_WIDTH - 1), per_b(1), per_b(POOL_BUF),
    ]
    out_shape = [
        jax.ShapeDtypeStruct((b, t, 3 * GW), BF16),
        jax.ShapeDtypeStruct((b, N_HEADS, t, HEAD_LANES), BF16),
        jax.ShapeDtypeStruct((b, N_HEADS, t, HEAD_LANES), BF16),
        jax.ShapeDtypeStruct((b, t // tm, GW, tm), BF16),
        jax.ShapeDtypeStruct((b, t, GW), F32),
        jax.ShapeDtypeStruct((b, t, QK_ROPE), F32),
        jax.ShapeDtypeStruct((b, CONV_WIDTH - 1, GW), F32),
        jax.ShapeDtypeStruct((b, 1, GW), F32),
        jax.ShapeDtypeStruct((b, POOL_BUF, GW), F32),
    ]
    return pl.pallas_call(
        functools.partial(_premix_prompt_kernel, tm=tm),
        grid=grid, in_specs=in_specs, out_specs=out_specs, out_shape=out_shape,
        scratch_shapes=[pltpu.VMEM((tm + 8, GW), F32), pltpu.VMEM((tm + 16, GW), F32),
                        pltpu.VMEM((1, GW), F32)],
        compiler_params=_cparams(("arbitrary", "arbitrary")),
        name="premix_prompt",
    )(x, mod, lw["w_in"], lw["sgu_w"], lw["sgu_bias"], lw["vec"], lw["w_gate"], lw["b_gate"],
      lw["w_qh"], lw["w_kv"], lw["w_pool"], rope)


def _attn_prompt_kernel(q_ref, k_ref, vt_ref, o_ref, m_ref, l_ref, acc_ref, s_ref, *, tq):
    qi = pl.program_id(1)
    cols = N_HEADS * tq
    m_ref[...] = jnp.full((1, cols), -jnp.inf, F32)
    l_ref[...] = jnp.zeros((1, cols), F32)
    acc_ref[...] = jnp.zeros((GW, tq), F32)
    qs = [q_ref[0, hh] for hh in range(N_HEADS)]

    def scores(j):
        start = pl.multiple_of(j * tq, tq)
        return jnp.concatenate(
            [_dot_nt(k_ref[0, hh, pl.ds(start, tq), :], qs[hh]) for hh in range(N_HEADS)], 1)

    def accumulate(j, st):
        m_old = m_ref[...]
        m_new = jnp.maximum(m_old, jnp.max(st, 0, keepdims=True))
        alpha = jnp.exp(m_old - m_new)
        p = jnp.exp(st - m_new)
        l_ref[...] = alpha * l_ref[...] + jnp.sum(p, 0, keepdims=True)
        m_ref[...] = m_new
        pb = p.astype(BF16)
        vt = vt_ref[0, j]
        for hh in range(N_HEADS):
            r = slice(hh * HEAD_DIM, (hh + 1) * HEAD_DIM)
            c = slice(hh * tq, (hh + 1) * tq)
            acc_ref[r, :] = alpha[:, c] * acc_ref[r, :] + _dot(vt[r, :], pb[:, c])

    s_ref[...] = scores(0)

    def body(j, carry):
        st = s_ref[...]
        s_next = scores(j + 1)
        accumulate(j, st)
        s_ref[...] = s_next
        return carry

    lax.fori_loop(0, qi, body, 0)
    kpos = lax.broadcasted_iota(jnp.int32, (tq, cols), 0)
    qpos = _mod(lax.broadcasted_iota(jnp.int32, (tq, cols), 1), tq)
    accumulate(qi, jnp.where(kpos <= qpos, s_ref[...], -jnp.inf))
    linv = 1.0 / l_ref[...]
    out = jnp.concatenate(
        [acc_ref[hh * HEAD_DIM:(hh + 1) * HEAD_DIM, :] * linv[:, hh * tq:(hh + 1) * tq] for hh in range(N_HEADS)], 0)
    o_ref[0] = out.T.astype(BF16)


def _attn_prompt(qcat, kcat, vt):
    b, _, t, _ = qcat.shape
    tq = ATTN_TILE
    assert t % tq == 0
    return pl.pallas_call(
        functools.partial(_attn_prompt_kernel, tq=tq),
        grid=(b, t // tq),
        in_specs=[pl.BlockSpec((1, N_HEADS, tq, HEAD_LANES), lambda i, j: (i, 0, j, 0)),
                  pl.BlockSpec((1, N_HEADS, t, HEAD_LANES), lambda i, j: (i, 0, 0, 0)),
                  pl.BlockSpec((1, t // tq, GW, tq), lambda i, j: (i, 0, 0, 0))],
        out_specs=pl.BlockSpec((1, tq, GW), lambda i, j: (i, j, 0)),
        out_shape=jax.ShapeDtypeStruct((b, t, GW), BF16),
        scratch_shapes=[pltpu.VMEM((1, N_HEADS * tq), F32), pltpu.VMEM((1, N_HEADS * tq), F32),
                        pltpu.VMEM((GW, tq), F32), pltpu.VMEM((tq, N_HEADS * tq), F32)],
        compiler_params=_cparams(("arbitrary", "arbitrary")),
        name="attn_prompt",
    )(qcat, kcat, vt)


def _premix_sample_kernel(x_ref, mod_ref, win_ref, sguc_ref, sgub_ref, vec_ref, wgate_ref, bgate_ref,
                          wq_ref, wuk_ref, wpool_ref, cos_ref, sin_ref, cbuf_ref, h0_ref, pbuf_ref,
                          abd_ref, qlat_ref, qrope_ref, ckvf_ref, ckvb_ref, krf_ref, krb_ref, vrows_ref,
                          nconv_ref, nh_ref, npool_ref, *, steps, start):
    def vec(i):
        return vec_ref[i:i + 1, :]

    sh1 = mod_ref[:, 0:D_MODEL]
    sc1 = mod_ref[:, D_MODEL:2 * D_MODEL]
    win = win_ref[...]
    projs = [_dot((_ln(x_ref[t]) * (1.0 + sc1) + sh1).astype(BF16), win) for t in range(steps)]

    vgs = []
    for t in range(steps):
        vg = _ln(_gelu(projs[t][:, OFF_V:OFF_V + GW])) * vec(V_SGU_G) + vec(V_SGU_B)
        vrows_ref[t] = vg
        vgs.append(vg)
    for t in range(steps):
        mixed = sgub_ref[t:t + 1, :]
        for s in range(t + 1):
            mixed = mixed + sguc_ref[t, s:s + 1, :] * vgs[s]
        abd_ref[t, :, 0:GW] = (_gelu(projs[t][:, OFF_U:OFF_U + GW]) * mixed).astype(BF16)

    z = [cbuf_ref[k] for k in range(CONV_WIDTH - 1)] + [p[:, OFF_XB:OFF_XB + GW] for p in projs]
    hstate = h0_ref[...]
    for t in range(steps):
        xc = vec(V_CONV_B) + z[t] * vec(V_CONV_W)
        for k in range(1, CONV_WIDTH):
            xc = xc + z[t + k] * vec(V_CONV_W + k)
        a, bt = _lru_gates(xc, wgate_ref[...], bgate_ref[...], vec(V_LAM))
        hstate = a * hstate + bt
        abd_ref[t, :, GW:2 * GW] = (hstate * _gelu(projs[t][:, OFF_GB:OFF_GB + GW])).astype(BF16)
    nh_ref[...] = hstate
    for k in range(CONV_WIDTH - 1):
        nconv_ref[k] = z[steps + k]

    for t in range(steps):
        cos = cos_ref[t:t + 1, :]
        sin = sin_ref[t:t + 1, :]
        p = projs[t]
        qlat, qr = _mla_queries(p[:, OFF_CQ:OFF_CQ + GW], vec(V_QNORM), wq_ref[...], wuk_ref[...], cos, sin)
        qlat_ref[t] = qlat.astype(BF16)
        qrope_ref[t] = qr.astype(BF16)
        ckv = _rms(p[:, OFF_CKV:OFF_CKV + GW]) * vec(V_KVNORM)
        ckvf_ref[t] = ckv
        ckvb_ref[t] = ckv.astype(BF16)
        kr = (p[:, OFF_KRA:OFF_KRA + ROPE_LANES] * cos + p[:, OFF_KRB:OFF_KRB + ROPE_LANES] * sin)[:, 0:QK_ROPE]
        krf_ref[t] = kr
        krb_ref[t] = kr.astype(BF16)

    zp = [pbuf_ref[k] for k in range(POOL_BUF)] + [p[:, OFF_PD:OFF_PD + GW] for p in projs]
    wins = _pool_windows()
    for t in range(steps):
        sums = []
        run = zp[POOL_BUF + t]
        for j in range(1, POOL_WINDOWS[-1]):
            if j in POOL_WINDOWS:
                sums.append(run)
            run = run + zp[POOL_BUF + t - j]
        sums.append(run)
        cnt = jnp.minimum(float(start + t + 1), wins)
        d = (_pool_select(*sums) / cnt - zp[POOL_BUF + t]).astype(BF16)
        abd_ref[t, :, 2 * GW:3 * GW] = (_dot(d, wpool_ref[...]) * vec(V_PSCALE)).astype(BF16)
    for k in range(POOL_BUF):
        npool_ref[k] = zp[steps + k]


def _premix_sample(x_tm, mod, lw, sgu_coef, sgu_bias, cos, sin, cbuf, h0, pbuf, start):
    steps, bd, d = x_tm.shape
    assert steps <= min(CHUNK, NEW_KEY_PAD)
    f = lambda w, dt: jax.ShapeDtypeStruct((steps, bd, w), dt)
    out_shape = [f(3 * GW, BF16), f(N_HEADS * GW, BF16), f(ROPE_LANES, BF16), f(GW, F32), f(GW, BF16),
                 f(QK_ROPE, F32), f(QK_ROPE, BF16), f(GW, F32),
                 jax.ShapeDtypeStruct((CONV_WIDTH - 1, bd, GW), F32),
                 jax.ShapeDtypeStruct((bd, GW), F32),
                 jax.ShapeDtypeStruct((POOL_BUF, bd, GW), F32)]
    args = (x_tm, mod, lw["w_in"], sgu_coef, sgu_bias, lw["vec"], lw["w_gate"], lw["b_gate"],
            lw["w_q"], lw["w_uk"], lw["w_pool"], cos, sin, cbuf, h0, pbuf)
    in_specs = [_full(a.shape) for a in args]
    in_specs[1] = pl.BlockSpec((bd, 2 * d), lambda i: (0, 0))
    return pl.pallas_call(
        functools.partial(_premix_sample_kernel, steps=steps, start=start),
        grid=(1,), in_specs=in_specs,
        out_specs=[_full(s.shape) for s in out_shape], out_shape=out_shape,
        compiler_params=_cparams(("arbitrary",)),
        name="premix_sample",
    )(*args)


def _decode_kernel(pt_ref, ql_ref, qr_ref, nk_ref, nkr_ref, ckv_hbm, krt_hbm, o_ref,
                   m_ref, l_ref, acc_ref, kb_ref, sb_ref, kin_ref, krin_ref, sem, *, layer, pages, steps):
    b = pl.program_id(0)
    j = pl.program_id(1)
    nj = pl.num_programs(1)
    rows = N_HEADS * steps
    slot = j & 1
    g = b * nj + j
    land = g & 1

    def page_copies(seq, grp, buf):
        out = []
        for i in range(pages):
            page = pt_ref[seq, grp * pages + i]
            out.append(pltpu.make_async_copy(ckv_hbm.at[layer, page], kin_ref.at[buf, i], sem.at[buf]))
            out.append(pltpu.make_async_copy(krt_hbm.at[layer, page], krin_ref.at[buf, i], sem.at[buf]))
        return out

    def start_all(copies):
        for n, c in enumerate(copies):
            c.start(priority=n // 2 % 2)

    @pl.when(g == 0)
    def _():
        start_all(page_copies(0, 0, 0))

    @pl.when(g + 1 < pl.num_programs(0) * nj)
    def _():
        wrap = j + 1 == nj
        start_all(page_copies(jnp.where(wrap, b + 1, b), jnp.where(wrap, 0, j + 1), 1 - land))

    for c in page_copies(b, j, land):
        c.wait()

    @pl.when(j == 0)
    def _():
        m_ref[...] = jnp.full((rows, 1), MASK_FLOOR, F32)
        l_ref[...] = jnp.zeros((rows, 1), F32)
        acc_ref[...] = jnp.zeros((rows, GW), F32)
        sb_ref[...] = jnp.full(sb_ref.shape, -jnp.inf, F32)
        kb_ref[1] = jnp.zeros(kb_ref.shape[1:], BF16)

    ql = ql_ref[0]
    qr = qr_ref[0]

    def update(s, values):
        m_old = m_ref[...]
        m_new = jnp.maximum(m_old, jnp.max(s, -1, keepdims=True))
        alpha = jnp.exp(m_old - m_new)
        p = jnp.exp(s - m_new)
        l_ref[...] = alpha * l_ref[...] + jnp.sum(p, -1, keepdims=True)
        acc_ref[...] = alpha * acc_ref[...] + _dot(p.astype(BF16), values)
        m_ref[...] = m_new

    parts = []
    for i in range(pages):
        k = kin_ref[land, i].astype(BF16)
        kb_ref[slot, i * PAGE_SIZE:(i + 1) * PAGE_SIZE, :] = k
        parts.append(_dot_nt(ql, k) + _dot(qr, krin_ref[land, i].astype(BF16)))
    s_now = jnp.concatenate(parts, -1)
    update(sb_ref[...], kb_ref[1 - slot])
    sb_ref[...] = s_now

    @pl.when(j == pl.num_programs(1) - 1)
    def _():
        update(sb_ref[...], kb_ref[slot])
        nk = nk_ref[0]
        s_new = _dot_nt(ql, nk) + _dot_nt(qr, nkr_ref[0])
        qstep = _mod(lax.broadcasted_iota(jnp.int32, (rows, NEW_KEY_PAD), 0), steps)
        kstep = lax.broadcasted_iota(jnp.int32, (rows, NEW_KEY_PAD), 1)
        update(jnp.where(kstep <= qstep, s_new, -jnp.inf), nk)
        o_ref[0] = acc_ref[...] / l_ref[...]


def _decode_attn(layer, page_table, cache_ckv, cache_krt, qlat, qrope, nk, nkr):
    bd, rows, _ = qlat.shape
    n_pages = page_table.shape[1]
    pages = min(DECODE_PAGES, n_pages)
    assert n_pages % pages == 0
    steps = rows // N_HEADS

    per_seq = lambda r, w: pl.BlockSpec((1, r, w), lambda i, j, pt: (i, 0, 0))
    in_specs = [per_seq(rows, GW), per_seq(rows, QK_ROPE), per_seq(NEW_KEY_PAD, GW), per_seq(NEW_KEY_PAD, QK_ROPE),
                pl.BlockSpec(memory_space=pl.ANY), pl.BlockSpec(memory_space=pl.ANY)]
    grid_spec = pltpu.PrefetchScalarGridSpec(
        num_scalar_prefetch=1, grid=(bd, n_pages // pages),
        in_specs=in_specs, out_specs=per_seq(rows, GW),
        scratch_shapes=[pltpu.VMEM((rows, 1), F32), pltpu.VMEM((rows, 1), F32), pltpu.VMEM((rows, GW), F32),
                        pltpu.VMEM((2, pages * PAGE_SIZE, GW), BF16),
                        pltpu.VMEM((rows, pages * PAGE_SIZE), F32),
                        pltpu.VMEM((2, pages, PAGE_SIZE, GW), F32),
                        pltpu.VMEM((2, pages, QK_ROPE, PAGE_SIZE), F32),
                        pltpu.SemaphoreType.DMA((2,))])
    return pl.pallas_call(
        functools.partial(_decode_kernel, layer=layer, pages=pages, steps=steps),
        grid_spec=grid_spec,
        out_shape=jax.ShapeDtypeStruct((bd, rows, GW), F32),
        compiler_params=_cparams(("arbitrary", "arbitrary")),
        name="decode_attn",
    )(page_table, qlat, qrope, nk, nkr, cache_ckv, cache_krt)


def _uv_kernel(o_ref, wuv_ref, out_ref):
    out = _dot(o_ref[0].astype(BF16), wuv_ref[0])
    for hh in range(1, N_HEADS):
        out = out + _dot(o_ref[hh].astype(BF16), wuv_ref[hh])
    out_ref[...] = out.astype(BF16)


def _uv_project(o_heads, wuv):
    _, n, _ = o_heads.shape
    return pl.pallas_call(
        _uv_kernel, grid=(1,),
        in_specs=[_full(o_heads.shape), _full(wuv.shape)],
        out_specs=_full((n, GW)), out_shape=jax.ShapeDtypeStruct((n, GW), BF16),
        compiler_params=_cparams(("arbitrary",)), name="uv_project",
    )(o_heads, wuv)


def _mix_norm(x_ref, abd_ref, c_ref, moda_ref, modb_ref, wabd_ref, wc_ref, lnv_ref):
    mix = _dot(abd_ref[...], wabd_ref[...]) + _dot(c_ref[...], wc_ref[...])
    g1 = moda_ref[0, :, 0:D_MODEL]
    sh2 = moda_ref[0, :, D_MODEL:2 * D_MODEL]
    sc2 = modb_ref[0, :, 0:D_MODEL]
    x1 = _ln(ALPHA * x_ref[...] + g1 * mix) * lnv_ref[0:1, :] + lnv_ref[1:2, :]
    h2 = (_ln(x1) * (1.0 + sc2) + sh2).astype(BF16)
    return x1, h2


def _swiglu(h2, w1, w3, w2):
    return _dot((_silu(_dot(h2, w1)) * _dot(h2, w3)).astype(BF16), w2)


def _post_dense_kernel(x_ref, abd_ref, c_ref, moda_ref, modb_ref, wabd_ref, wc_ref, lnv_ref,
                       w1_ref, w3_ref, w2_ref, o_ref):
    x1, h2 = _mix_norm(x_ref, abd_ref, c_ref, moda_ref, modb_ref, wabd_ref, wc_ref, lnv_ref)
    f = _swiglu(h2, w1_ref[...], w3_ref[...], w2_ref[...])
    g2 = modb_ref[0, :, D_MODEL:2 * D_MODEL]
    o_ref[...] = _ln(ALPHA * x1 + g2 * f) * lnv_ref[2:3, :] + lnv_ref[3:4, :]


def _token_specs(tm, tiles_per_group, mod_rows):
    mi = lambda blk: (lambda i, *_: (i // tiles_per_group, 0, blk))
    tok = lambda w: pl.BlockSpec((tm, w), lambda i, *_: (i, 0))
    mod = lambda blk: pl.BlockSpec((1, mod_rows, 2 * D_MODEL), mi(blk))
    return tok, mod


def _post_dense(x, abd, c, mod, lw, fw, tm, tiles_per_group):
    n, d = x.shape
    tok, modspec = _token_specs(tm, tiles_per_group, mod.shape[1])
    weights = (lw["w_abd"], lw["w_c"], lw["lnv"], fw["w1"], fw["w3"], fw["w2"])
    return pl.pallas_call(
        _post_dense_kernel, grid=(n // tm,),
        in_specs=[tok(d), tok(3 * GW), tok(GW), modspec(1), modspec(2)] + [_full(w.shape) for w in weights],
        out_specs=tok(d), out_shape=jax.ShapeDtypeStruct((n, d), F32),
        compiler_params=_cparams(("arbitrary",)), name="post_dense",
    )(x, abd, c, mod, mod, *weights)


def _post_route_kernel(x_ref, abd_ref, c_ref, moda_ref, modb_ref, wabd_ref, wc_ref, lnv_ref,
                       wr_ref, br_ref, x1_ref, h2_ref, route_ref, cnt_ref, run_ref):
    tm = x_ref.shape[0]

    @pl.when(pl.program_id(0) == 0)
    def _():
        run_ref[...] = jnp.zeros((1, 128), F32)

    x1, h2 = _mix_norm(x_ref, abd_ref, c_ref, moda_ref, modb_ref, wabd_ref, wc_ref, lnv_ref)
    x1_ref[...] = x1
    h2_ref[...] = h2
    lane = lax.broadcasted_iota(jnp.int32, (1, 128), 1)
    logits = jnp.where(lane < N_EXPERTS, _dot(h2, wr_ref[...]) + br_ref[...], -jnp.inf)
    m1 = jnp.max(logits, -1, keepdims=True)
    i1 = jnp.min(jnp.where(logits == m1, lane, 128), -1, keepdims=True)
    rest = jnp.where(lane == i1, -jnp.inf, logits)
    m2 = jnp.max(rest, -1, keepdims=True)
    i2 = jnp.min(jnp.where(rest == m2, lane, 128), -1, keepdims=True)
    e2 = jnp.exp(m2 - m1)
    gate1 = 1.0 / (1.0 + e2)
    gate2 = e2 / (1.0 + e2)
    sel = jnp.where((lane == i1) | (lane == i2), 1.0, 0.0)
    r = lax.broadcasted_iota(jnp.int32, (tm, tm), 0)
    c = lax.broadcasted_iota(jnp.int32, (tm, tm), 1)
    before = _dot(jnp.where(c < r, 1.0, 0.0).astype(BF16), sel.astype(BF16)) + run_ref[...]
    rank1 = jnp.sum(jnp.where(lane == i1, before, 0.0), -1, keepdims=True)
    rank2 = jnp.sum(jnp.where(lane == i2, before, 0.0), -1, keepdims=True)
    run_ref[...] = run_ref[...] + jnp.sum(sel, 0, keepdims=True)
    cnt_ref[...] = run_ref[...]
    cols = (i1.astype(F32), i2.astype(F32), gate1, gate2, rank1, rank2)
    route = jnp.zeros((tm, 128), F32)
    for k, v in enumerate(cols):
        route = jnp.where(lane == k, v, route)
    route_ref[...] = route


def _post_route(x, abd, c, mod, lw, mw, tm, tiles_per_group):
    n, d = x.shape
    tok, modspec = _token_specs(tm, tiles_per_group, mod.shape[1])
    weights = (lw["w_abd"], lw["w_c"], lw["lnv"], mw["w_router"], mw["b_router"])
    return pl.pallas_call(
        _post_route_kernel, grid=(n // tm,),
        in_specs=[tok(d), tok(3 * GW), tok(GW), modspec(1), modspec(2)] + [_full(w.shape) for w in weights],
        out_specs=[tok(d), tok(d), tok(128), _full((1, 128))],
        out_shape=[jax.ShapeDtypeStruct((n, d), F32), jax.ShapeDtypeStruct((n, d), BF16),
                   jax.ShapeDtypeStruct((n, 128), F32), jax.ShapeDtypeStruct((1, 128), F32)],
        scratch_shapes=[pltpu.VMEM((1, 128), F32)],
        compiler_params=_cparams(("arbitrary",)), name="post_route",
    )(x, abd, c, mod, mod, *weights)


def _moe_group_kernel(tile_ref, blk_ref, first_ref, last_ref, valid_ref, texp_ref,
                      h2_ref, post_ref, w1_ref, w3_ref, w2_ref, out_ref, xg_ref):
    del blk_ref, texp_ref
    w = pl.program_id(0)

    @pl.when(first_ref[w] == 1)
    def _():
        xg_ref[...] = jnp.zeros_like(xg_ref)

    @pl.when(valid_ref[w] == 1)
    def _():
        row = tile_ref[w] * MOE_TILE + lax.broadcasted_iota(jnp.int32, (MOE_TILE, 1), 0)
        pick = (post_ref[0:1, :] == row) | (post_ref[1:2, :] == row)
        xg_ref[...] += _dot(jnp.where(pick, 1.0, 0.0).astype(BF16), h2_ref[...])

    @pl.when(last_ref[w] == 1)
    def _():
        out_ref[...] = _swiglu(xg_ref[...].astype(BF16), w1_ref[...], w3_ref[...], w2_ref[...]).astype(BF16)

    @pl.when(valid_ref[w] == 0)
    def _():
        out_ref[...] = jnp.zeros_like(out_ref)


def _moe_group(items, item_expert, h2, pos_t, mw, n_rows):
    n, d = h2.shape
    n_items = items[0].shape[0]
    wspec = lambda shape: pl.BlockSpec((None,) + shape, lambda w, tile, blk, f, l, v, te: (te[w], 0, 0))
    grid_spec = pltpu.PrefetchScalarGridSpec(
        num_scalar_prefetch=6, grid=(n_items,),
        in_specs=[pl.BlockSpec((MOE_BLOCK, d), lambda w, tile, blk, f, l, v, te: (blk[w], 0)),
                  pl.BlockSpec((2, MOE_BLOCK), lambda w, tile, blk, f, l, v, te: (0, blk[w])),
                  wspec((d, D_FF)), wspec((d, D_FF)), wspec((D_FF, d))],
        out_specs=pl.BlockSpec((MOE_TILE, d), lambda w, tile, blk, f, l, v, te: (tile[w], 0)),
        scratch_shapes=[pltpu.VMEM((MOE_TILE, d), F32)])
    return pl.pallas_call(
        _moe_group_kernel, grid_spec=grid_spec,
        out_shape=jax.ShapeDtypeStruct((n_rows, d), BF16),
        compiler_params=_cparams(("arbitrary",)), name="moe_group",
    )(*items, item_expert, h2, pos_t, mw["w1"], mw["w3"], mw["w2"])


def _moe_combine_kernel(blk_ref, tile_ref, first_ref, last_ref, valid_ref,
                        out_ref, pos_ref, gate_ref, y_ref, acc_ref):
    del blk_ref
    w = pl.program_id(0)

    @pl.when(first_ref[w] == 1)
    def _():
        acc_ref[...] = jnp.zeros_like(acc_ref)

    @pl.when(valid_ref[w] == 1)
    def _():
        row = tile_ref[w] * MOE_TILE + lax.broadcasted_iota(jnp.int32, (1, MOE_TILE), 1)
        sel = (jnp.where(pos_ref[:, 0:1] == row, gate_ref[:, 0:1], 0.0)
               + jnp.where(pos_ref[:, 1:2] == row, gate_ref[:, 1:2], 0.0))
        acc_ref[...] += _dot(sel.astype(BF16), out_ref[...])

    @pl.when(last_ref[w] == 1)
    def _():
        y_ref[...] = acc_ref[...]


def _moe_combine(items, out_sorted, pos, gates):
    n = pos.shape[0]
    d = out_sorted.shape[1]
    n_items = items[0].shape[0]
    tokb = lambda w_: pl.BlockSpec((MOE_BLOCK, w_), lambda w, blk, tile, f, l, v: (blk[w], 0))
    grid_spec = pltpu.PrefetchScalarGridSpec(
        num_scalar_prefetch=5, grid=(n_items,),
        in_specs=[pl.BlockSpec((MOE_TILE, d), lambda w, blk, tile, f, l, v: (tile[w], 0)), tokb(2), tokb(2)],
        out_specs=tokb(d),
        scratch_shapes=[pltpu.VMEM((MOE_BLOCK, d), F32)])
    return pl.pallas_call(
        _moe_combine_kernel, grid_spec=grid_spec,
        out_shape=jax.ShapeDtypeStruct((n, d), F32),
        compiler_params=_cparams(("arbitrary",)), name="moe_combine",
    )(*items, out_sorted, pos, gates)


def _moe_final_kernel(x1_ref, y_ref, modb_ref, lnv_ref, o_ref):
    g2 = modb_ref[0, :, D_MODEL:2 * D_MODEL]
    o_ref[...] = _ln(ALPHA * x1_ref[...] + g2 * y_ref[...]) * lnv_ref[2:3, :] + lnv_ref[3:4, :]


def _moe_final(x1, y, y_row0, mod, lw, tm, tiles_per_group):
    n, d = x1.shape
    assert y_row0 % tm == 0
    tok, modspec = _token_specs(tm, tiles_per_group, mod.shape[1])
    return pl.pallas_call(
        _moe_final_kernel, grid=(n // tm,),
        in_specs=[tok(d), pl.BlockSpec((tm, d), lambda i: (i + y_row0 // tm, 0)), modspec(2),
                  _full(lw["lnv"].shape)],
        out_specs=tok(d), out_shape=jax.ShapeDtypeStruct((n, d), F32),
        compiler_params=_cparams(("arbitrary",)), name="moe_final",
    )(x1, y, mod, lw["lnv"])


def _pair_items(incidence, n_items):
    n_cols = incidence.shape[1]
    running = jnp.cumsum(incidence.reshape(-1).astype(jnp.int32))
    n_valid = running[-1]
    idx = jnp.arange(n_items)
    flat = jnp.minimum(jnp.searchsorted(running, idx + 1, side="left", method="compare_all"), running.shape[0] - 1)
    rows, cols = flat // n_cols, flat % n_cols
    valid = idx < n_valid
    last_valid = jnp.maximum(n_valid - 1, 0)
    rows = jnp.where(valid, rows, rows[last_valid]).astype(jnp.int32)
    cols = jnp.where(valid, cols, cols[last_valid]).astype(jnp.int32)
    prev_rows = jnp.concatenate([rows[:1] - 1, rows[:-1]])
    next_rows = jnp.concatenate([rows[1:], rows[-1:] + 1])
    first = valid & (rows != prev_rows)
    last = valid & ((rows != next_rows) | (idx == last_valid))
    return rows, cols, first.astype(jnp.int32), last.astype(jnp.int32), valid.astype(jnp.int32)


def _moe_ffn(h2, route, counts, mw):
    n, d = h2.shape
    assert n % MOE_BLOCK == 0
    n_blocks = n // MOE_BLOCK
    n_tiles = (2 * n + N_EXPERTS * (MOE_TILE - 1)) // MOE_TILE + 1
    n_items = n_tiles + N_EXPERTS * n_blocks
    experts = route[:, 0:2].astype(jnp.int32)
    gates = route[:, 2:4]
    tiles_per_expert = (counts.astype(jnp.int32) + MOE_TILE - 1) // MOE_TILE
    tile_end = jnp.cumsum(tiles_per_expert)
    row_start = (tile_end - tiles_per_expert) * MOE_TILE
    pos = row_start[experts] + route[:, 4:6].astype(jnp.int32)
    tile_expert = jnp.minimum(jnp.searchsorted(tile_end, jnp.arange(n_tiles), side="right"),
                              N_EXPERTS - 1).astype(jnp.int32)
    pick_tile = (pos // MOE_TILE).reshape(n_blocks, 1, 2 * MOE_BLOCK)
    of_expert = experts.reshape(n_blocks, 1, 2 * MOE_BLOCK) == jnp.arange(N_EXPERTS)[None, :, None]
    lo = jnp.min(jnp.where(of_expert, pick_tile, n_tiles), -1)[..., None]
    hi = jnp.max(jnp.where(of_expert, pick_tile, -1), -1)[..., None]
    tiles = jnp.arange(n_tiles)[None, None, :]
    incidence = jnp.any((lo <= tiles) & (tiles <= hi), 1)
    tile_i, blk_i, first_i, last_i, valid_i = _pair_items(incidence.T, n_items)
    item_expert = tile_expert[tile_i]
    n_valid = jnp.sum(valid_i)
    spare_tile = jnp.minimum(tile_end[-1] + jnp.arange(n_items) - n_valid, n_tiles - 1)
    tile_i = jnp.where(valid_i == 1, tile_i, spare_tile).astype(jnp.int32)
    out_sorted = _moe_group((tile_i, blk_i, first_i, last_i, valid_i), item_expert, h2, pos.T, mw,
                            n_tiles * MOE_TILE)
    return _moe_combine(_pair_items(incidence, n_items), out_sorted, pos, gates)


def _block_diag(blocks):
    n, r, c = blocks.shape
    on_diag = jnp.eye(n, dtype=bool)[:, None, :, None]
    return jnp.where(on_diag, blocks[:, :, None, :], 0).reshape(n * r, n * c)


def _swap_rope_halves(w):
    half = QK_ROPE // 2
    return jnp.concatenate([w[..., half:], w[..., :half]], -1)


def _layer_weights(l, p):
    w_in = p["w_in"][l]
    s = np.cumsum([GW, GW, GW, GW, GW, GW, QK_ROPE])
    kr_cols = w_in[:, s[5]:s[6]]
    gap = jnp.zeros_like(kr_cols)
    kr_sw = _swap_rope_halves(kr_cols)
    w_in_r = jnp.concatenate(
        [w_in[:, :s[5]], w_in[:, s[6]:], kr_cols, gap, kr_cols, gap, kr_sw, gap, kr_sw, gap], 1).astype(BF16)
    vec_names = ("sgu_ln_g", "sgu_ln_b", "lru_conv_b", "lru_lambda", "mla_q_norm", "mla_kv_norm", "pool_scale")
    assert (V_SGU_G, V_SGU_B, V_CONV_B, V_LAM, V_QNORM, V_KVNORM, V_PSCALE) == tuple(range(len(vec_names)))
    vec = jnp.concatenate(
        [jnp.stack([p[name][l] for name in vec_names]), jnp.zeros((V_CONV_W - len(vec_names), GW), F32),
         p["lru_conv_w"][l], jnp.zeros((VEC_ROWS - V_CONV_W - CONV_WIDTH, GW), F32)], 0)
    w_uq = p["mla_w_uq"][l]
    q_rope_cols = w_uq[:, :, QK_NOPE:]
    w_q = jnp.concatenate([w_uq[:, :, :QK_NOPE].reshape(GW, GW), q_rope_cols.reshape(GW, ROPE_LANES),
                           _swap_rope_halves(q_rope_cols).reshape(GW, ROPE_LANES)], 1).astype(BF16)
    w_uk = _block_diag(jnp.transpose(p["mla_w_uk"][l], (1, 2, 0))).astype(BF16)
    w_uv = p["mla_w_uv"][l]
    zq = jnp.zeros((GW, N_HEADS, HEAD_LANES - QK_NOPE - QK_ROPE), F32)
    w_qh = jnp.concatenate(
        [jnp.concatenate([w_uq, zq], -1).reshape(GW, N_HEADS * HEAD_LANES),
         jnp.concatenate([jnp.zeros((GW, N_HEADS, QK_NOPE), F32), _swap_rope_halves(q_rope_cols), zq],
                         -1).reshape(GW, N_HEADS * HEAD_LANES)], 1).astype(BF16)
    w_kn = jnp.concatenate([p["mla_w_uk"][l], jnp.zeros((GW, N_HEADS, HEAD_LANES - QK_NOPE), F32)], -1)
    w_kv = jnp.concatenate([w_kn.reshape(GW, N_HEADS * HEAD_LANES), w_uv.reshape(GW, GW)], 1).astype(BF16)
    w_uv_pad = jnp.where(jnp.eye(N_HEADS, dtype=bool)[:, None, :, None], w_uv[None], 0).reshape(N_HEADS, GW, GW)
    w_out = p["w_out"][l]
    return {
        "w_in": w_in_r,
        "sgu_w": p["sgu_w"][l],
        "sgu_b": p["sgu_b"][l],
        "vec": vec,
        "w_gate": jnp.concatenate([_block_diag(p["lru_wa"][l]), _block_diag(p["lru_wx"][l])], 1).astype(BF16),
        "b_gate": jnp.concatenate([p["lru_ba"][l], p["lru_bx"][l]])[None, :],
        "w_q": w_q,
        "w_uk": w_uk,
        "w_qh": w_qh,
        "w_kv": w_kv,
        "w_uv": w_uv_pad.astype(BF16),
        "w_pool": _block_diag(p["pool_w"][l]).astype(BF16),
        "w_abd": jnp.concatenate([w_out[:2 * GW], w_out[3 * GW:]], 0).astype(BF16),
        "w_c": w_out[2 * GW:3 * GW].astype(BF16),
        "lnv": jnp.stack([p["ln1_g"][l], p["ln1_b"][l], p["ln2_g"][l], p["ln2_b"][l]]),
    }


def _rope_tables(start, steps):
    half = QK_ROPE // 2
    freq = ROPE_THETA ** (-jnp.arange(half, dtype=F32) / half)
    ang = (start + jnp.arange(steps)).astype(F32)[:, None] * freq[None, :]
    cos = jnp.cos(ang)
    sin = jnp.sin(ang)
    return jnp.concatenate([cos, cos], -1), jnp.concatenate([-sin, sin], -1)


def _rope_tiled(start, steps):
    cos, sin = _rope_tables(start, steps)
    return jnp.tile(cos, (1, N_HEADS)), jnp.tile(sin, (1, N_HEADS))


def _rope_head_tables(start, steps):
    cos, sin = _rope_tables(start, steps)
    one = jnp.ones((steps, QK_NOPE), F32)
    z64 = jnp.zeros((steps, QK_NOPE), F32)
    z32 = jnp.zeros((steps, QK_ROPE), F32)
    return jnp.concatenate([one, cos, z32, z64, sin, z32, cos, z32, cos, z32, sin, z32, sin, z32], 1)


def kernel(x_prompt, x_sample, c_prompt, c_sample, cache_ckv, cache_kr, page_table, state_lru_conv, state_lru_h, state_pool, ada_w, ada_b, w_in, w_out, ln1_g, ln1_b, ln2_g, ln2_b, sgu_ln_g, sgu_ln_b, sgu_w, sgu_b, lru_conv_w, lru_conv_b, lru_wa, lru_ba, lru_wx, lru_bx, lru_lambda, mla_q_norm, mla_w_uq, mla_kv_norm, mla_w_uk, mla_w_uv, pool_w, pool_scale, ffn_w1, ffn_w3, ffn_w2, moe_router, moe_router_b, moe_w1, moe_w3, moe_w2):
    p = dict(w_in=w_in, w_out=w_out, ln1_g=ln1_g, ln1_b=ln1_b, ln2_g=ln2_g, ln2_b=ln2_b,
             sgu_ln_g=sgu_ln_g, sgu_ln_b=sgu_ln_b, sgu_w=sgu_w, sgu_b=sgu_b, lru_conv_w=lru_conv_w,
             lru_conv_b=lru_conv_b, lru_wa=lru_wa, lru_ba=lru_ba, lru_wx=lru_wx, lru_bx=lru_bx,
             lru_lambda=lru_lambda, mla_q_norm=mla_q_norm, mla_w_uq=mla_w_uq, mla_kv_norm=mla_kv_norm,
             mla_w_uk=mla_w_uk, mla_w_uv=mla_w_uv, pool_w=pool_w, pool_scale=pool_scale)
    bp, tp, d = x_prompt.shape
    bd, td, _ = x_sample.shape
    n_pages = page_table.shape[1]
    past_len = n_pages * PAGE_SIZE
    tmp = PROMPT_TILE

    rope_p = _rope_head_tables(0, tp)
    cos_s, sin_s = _rope_tiled(past_len, td)
    cache_krt = jnp.swapaxes(cache_kr, 2, 3)
    pad = (-(bd + bp)) % 8
    c_all = jnp.concatenate([c_sample, c_prompt, jnp.zeros((pad, d), F32)], 0)

    xp = x_prompt
    xs = jnp.transpose(x_sample, (1, 0, 2))
    sp = [[] for _ in range(5)]
    ss = [[] for _ in range(6)]
    for l in range(DEPTH):
        lw = _layer_weights(l, p)
        mod = _ada(c_all, ada_w[l].astype(BF16), ada_b[l][None, :])
        mod_s = mod[:bd]
        mod_p = mod[bd:bd + bp][:, None, :]

        (abd_p, qcat_p, kcat_p, vt_p, ckv_p, kr_p, nconv_p, nh_p, npool_p) = _premix_prompt(
            xp, mod_p, dict(lw, sgu_w=lw["sgu_w"][:, :CHUNK, :CHUNK],
                            sgu_bias=jnp.repeat(lw["sgu_b"][:, :CHUNK].T, HEAD_DIM, axis=1)), rope_p)
        c_p = _attn_prompt(qcat_p, kcat_p, vt_p)

        sgu_coef = jnp.repeat(jnp.transpose(lw["sgu_w"][:, :td, :td], (1, 2, 0)), HEAD_DIM, axis=2)
        sgu_bias_s = jnp.repeat(lw["sgu_b"][:, :td].T, HEAD_DIM, axis=1)
        (abd_s, qlat_s, qrope_s, ckv_s, ckvb_s, kr_s, krb_s, vrows_s, nconv_s, nh_s, npool_s) = _premix_sample(
            xs, mod_s, lw, sgu_coef, sgu_bias_s, cos_s, sin_s,
            jnp.transpose(state_lru_conv[l], (1, 0, 2)), state_lru_h[l],
            jnp.transpose(state_pool[l], (1, 0, 2)), past_len)
        to_rows = lambda a, w: jnp.transpose(a.reshape(td, bd, N_HEADS, w), (1, 2, 0, 3)).reshape(bd, N_HEADS * td, w)
        pad_keys = lambda a: jnp.pad(jnp.transpose(a, (1, 0, 2)), ((0, 0), (0, NEW_KEY_PAD - td), (0, 0)))
        o_s = _decode_attn(l, page_table, cache_ckv, cache_krt, to_rows(qlat_s, GW), to_rows(qrope_s, QK_ROPE),
                           pad_keys(ckvb_s), pad_keys(krb_s))
        o_heads = jnp.transpose(o_s.reshape(bd, N_HEADS, td, GW), (1, 2, 0, 3)).reshape(N_HEADS, td * bd, GW)
        c_s = _uv_project(o_heads, lw["w_uv"])

        xp2 = xp.reshape(bp * tp, d)
        xs2 = xs.reshape(td * bd, d)
        args_p = (xp2, abd_p.reshape(bp * tp, 3 * GW), c_p.reshape(bp * tp, GW), mod_p)
        args_s = (xs2, abd_s.reshape(td * bd, 3 * GW), c_s, mod_s[None])
        j = l // 2
        if l % 2 == 0:
            fw = {"w1": ffn_w1[j].astype(BF16), "w3": ffn_w3[j].astype(BF16), "w2": ffn_w2[j].astype(BF16)}
            xp2 = _post_dense(*args_p, lw, fw, tmp, tp // tmp)
            xs2 = _post_dense(*args_s, lw, fw, bd, td)
        else:
            mw = {"w_router": jnp.pad(moe_router[j], ((0, 0), (0, 128 - N_EXPERTS))).astype(BF16),
                  "b_router": jnp.pad(moe_router_b[j], (0, 128 - N_EXPERTS))[None, :],
                  "w1": moe_w1[j].astype(BF16), "w3": moe_w3[j].astype(BF16), "w2": moe_w2[j].astype(BF16)}
            x1p, h2p, route_p, cnt_p = _post_route(*args_p, lw, mw, tmp, tp // tmp)
            x1s, h2s, route_s, cnt_s = _post_route(*args_s, lw, mw, bd, td)
            before_s = cnt_p[0, :N_EXPERTS][route_s[:, 0:2].astype(jnp.int32)]
            route = jnp.concatenate([route_p, route_s.at[:, 4:6].add(before_s)], 0)
            y = _moe_ffn(jnp.concatenate([h2p, h2s], 0), route, (cnt_p + cnt_s)[0, :N_EXPERTS], mw)
            xp2 = _moe_final(x1p, y, 0, mod_p, lw, tmp, tp // tmp)
            xs2 = _moe_final(x1s, y, bp * tp, mod_s[None], lw, bd, td)
        xp = xp2.reshape(bp, tp, d)
        xs = xs2.reshape(td, bd, d)

        for k, v in enumerate((ckv_p, kr_p, nconv_p, nh_p[:, 0], npool_p)):
            sp[k].append(v)
        bt = lambda a: jnp.transpose(a, (1, 0, 2))
        for k, v in enumerate((bt(ckv_s), bt(kr_s), bt(nconv_s), nh_s, bt(npool_s), bt(vrows_s))):
            ss[k].append(v)

    return (xp, jnp.transpose(xs, (1, 0, 2)),
            jnp.stack(sp[0]), jnp.stack(sp[1]), jnp.stack(sp[2]), jnp.stack(sp[3]), jnp.stack(sp[4]),
            jnp.stack(ss[0]), jnp.stack(ss[1]), jnp.stack(ss[2]), jnp.stack(ss[3]), jnp.stack(ss[4]),
            jnp.stack(ss[5]))
```

```python
import functools

import jax
import jax.numpy as jnp
import numpy as np
from jax import lax
from jax.experimental import pallas as pl
from jax.experimental.pallas import tpu as pltpu

F32 = jnp.float32
BF16 = jnp.bfloat16

D_MODEL = 1024
DEPTH = 2
PAGE_SIZE = 128
GROUP_WIDTH = D_MODEL // 4
HEAD_DIM = 64
N_HEADS = GROUP_WIDTH // HEAD_DIM
CHUNK = 128
CONV_WIDTH = 4
LRU_C = 8.0
QK_NOPE = HEAD_DIM
QK_ROPE = HEAD_DIM // 2
MLA_SCALE = (QK_NOPE + QK_ROPE) ** -0.5
ROPE_THETA = 10000.0
POOL_WINDOWS = (2, 4, 8, 16)
POOL_BUF = 15
POOL_GC = GROUP_WIDTH // 4
D_FF = ((8 * D_MODEL // 3 + 127) // 128) * 128
N_EXPERTS = 8
ALPHA = (2 * DEPTH) ** 0.25
LN_EPS = 1e-5
RMS_EPS = 1e-6

GW = GROUP_WIDTH
OFF_U, OFF_V, OFF_XB, OFF_GB, OFF_CQ, OFF_CKV, OFF_PD, OFF_KRA, OFF_KRB = (
    0, GW, 2 * GW, 3 * GW, 4 * GW, 5 * GW, 6 * GW, 7 * GW, 7 * GW + 128)
P_IN_PAD = 8 * GW
ROPE_LANES = N_HEADS * QK_ROPE
HEAD_LANES = 128

V_SGU_G, V_SGU_B, V_CONV_B, V_LAM, V_QNORM, V_KVNORM, V_PSCALE = 0, 1, 2, 3, 4, 5, 6
V_CONV_W = 8
VEC_ROWS = 16

VMEM_LIMIT_BYTES = 56 * 1024 * 1024
PROMPT_TILE = 256
ATTN_TILE = 256
DECODE_PAGES = 16
DECODE_RING = 3
NEW_KEY_PAD = 16
MASK_FLOOR = -3.0e38
MOE_TILE = 256
MOE_BLOCK = 512


def _dot(a, b):
    return jnp.dot(a, b, preferred_element_type=F32)


def _dot_nt(a, b):
    return lax.dot_general(a, b, (((1,), (1,)), ((), ())), preferred_element_type=F32)


def _ln(x):
    mu = jnp.mean(x, -1, keepdims=True)
    xc = x - mu
    var = jnp.mean(xc * xc, -1, keepdims=True)
    return xc * lax.rsqrt(var + LN_EPS)


def _rms(x):
    return x * lax.rsqrt(jnp.mean(x * x, -1, keepdims=True) + RMS_EPS)


def _gelu(x):
    return 0.5 * x * (1.0 + jnp.tanh(0.7978845608028654 * (x + 0.044715 * (x * x * x))))


def _silu(x):
    return x * jax.nn.sigmoid(x)


def _softplus(x):
    return jnp.maximum(x, 0.0) + jnp.log1p(jnp.exp(-jnp.abs(x)))


def _lane_group(width, group):
    assert group & (group - 1) == 0
    return lax.broadcasted_iota(jnp.int32, (1, width), 1) >> (group.bit_length() - 1)


def _mod(x, n):
    return x & (n - 1) if n & (n - 1) == 0 else lax.rem(x, n)


def _cparams(sem):
    return pltpu.CompilerParams(dimension_semantics=sem, vmem_limit_bytes=VMEM_LIMIT_BYTES)


def _full(shape):
    n = len(shape)
    return pl.BlockSpec(shape, lambda *_: (0,) * n)


def _ada_kernel(c_ref, w_ref, b_ref, o_ref):
    o_ref[...] = _dot(_silu(c_ref[...]).astype(BF16), w_ref[...]) + b_ref[...]


def _ada(c_all, w, b):
    m, d = c_all.shape
    n = w.shape[1]
    tn = 512
    return pl.pallas_call(
        _ada_kernel,
        grid=(n // tn,),
        in_specs=[pl.BlockSpec((m, d), lambda j: (0, 0)),
                  pl.BlockSpec((d, tn), lambda j: (0, j)),
                  pl.BlockSpec((1, tn), lambda j: (0, j))],
        out_specs=pl.BlockSpec((m, tn), lambda j: (0, j)),
        out_shape=jax.ShapeDtypeStruct((m, n), F32),
        compiler_params=_cparams(("arbitrary",)),
        name="ada",
    )(c_all, w, b)


def _lru_gates(xc, wgate, bgate, lam):
    g = _dot(xc.astype(BF16), wgate) + bgate
    r = jax.nn.sigmoid(g[:, :GW])
    i = jax.nn.sigmoid(g[:, GW:])
    log_a = (-LRU_C) * r * _softplus(-lam)
    a = jnp.exp(log_a)
    bt = jnp.sqrt(-jnp.tanh(log_a) * (a * a + 1.0)) * i * xc
    return a, bt


def _mla_queries(cq, qnorm, wq, wuk, cos, sin):
    q = _dot((_rms(cq) * qnorm).astype(BF16), wq)
    qlat = _dot(q[:, :GW].astype(BF16), wuk) * MLA_SCALE
    qr = (q[:, GW:GW + ROPE_LANES] * cos + q[:, GW + ROPE_LANES:] * sin) * MLA_SCALE
    return qlat, qr


def _pool_select(s2, s4, s8, s16):
    grp = _lane_group(GW, POOL_GC)
    return jnp.where(grp == 0, s2, jnp.where(grp == 1, s4, jnp.where(grp == 2, s8, s16)))


def _pool_windows():
    grp = _lane_group(GW, POOL_GC)
    w = jnp.where(grp == 0, POOL_WINDOWS[0],
                  jnp.where(grp == 1, POOL_WINDOWS[1],
                            jnp.where(grp == 2, POOL_WINDOWS[2], POOL_WINDOWS[3])))
    return w.astype(F32)


def _premix_prompt_kernel(x_ref, mod_ref, win_ref, sguw_ref, sgub_ref, vec_ref, wgate_ref, bgate_ref,
                          wq_ref, wkv_ref, wpool_ref, rope_ref,
                          abd_ref, qcat_ref, kcat_ref, vt_ref, ckvf_ref, krf_ref,
                          nconv_ref, nh_ref, npool_ref,
                          zc_ref, zp_ref, hc_ref, *, tm):
    t = pl.program_id(1)

    @pl.when(t == 0)
    def _():
        zc_ref[0:8, :] = jnp.zeros((8, GW), F32)
        zp_ref[0:16, :] = jnp.zeros((16, GW), F32)
        hc_ref[...] = jnp.zeros((1, GW), F32)

    def vec(i):
        return vec_ref[i:i + 1, :]

    x = x_ref[0]
    sh1 = mod_ref[0, :, 0:D_MODEL]
    sc1 = mod_ref[0, :, D_MODEL:2 * D_MODEL]
    h = _ln(x) * (1.0 + sc1) + sh1
    proj = _dot(h.astype(BF16), win_ref[...])

    ug = _gelu(proj[:, OFF_U:OFF_U + GW])
    vg = _ln(_gelu(proj[:, OFF_V:OFF_V + GW])) * vec(V_SGU_G) + vec(V_SGU_B)
    vgb = vg.astype(BF16)
    head = _lane_group(GW, HEAD_DIM)
    row = lax.broadcasted_iota(jnp.int32, (CHUNK, CHUNK), 0)
    col = lax.broadcasted_iota(jnp.int32, (CHUNK, CHUNK), 1)
    wmix = [jnp.where(col <= row, sguw_ref[hh], 0.0).astype(BF16) for hh in range(N_HEADS)]
    parts = []
    for c in range(tm // CHUNK):
        vc = vgb[c * CHUNK:(c + 1) * CHUNK]
        acc = _dot(wmix[0], vc)
        for hh in range(1, N_HEADS):
            acc = jnp.where(head == hh, _dot(wmix[hh], vc), acc)
        parts.append(acc + sgub_ref[...])
    mixed = jnp.concatenate(parts, 0) if len(parts) > 1 else parts[0]
    abd_ref[0, :, 0:GW] = (ug * mixed).astype(BF16)

    xb = proj[:, OFF_XB:OFF_XB + GW]
    zc_ref[8:8 + tm, :] = xb
    z = zc_ref[...]
    xc = (vec(V_CONV_B) + vec(V_CONV_W + 3) * z + vec(V_CONV_W + 2) * pltpu.roll(z, 1, 0)
          + vec(V_CONV_W + 1) * pltpu.roll(z, 2, 0) + vec(V_CONV_W) * pltpu.roll(z, 3, 0))[8:]
    nconv_ref[0] = zc_ref[pl.ds(8 + tm - (CONV_WIDTH - 1), CONV_WIDTH - 1), :]
    zc_ref[0:8, :] = xb[tm - 8:tm]
    a, bt = _lru_gates(xc, wgate_ref[...], bgate_ref[...], vec(V_LAM))
    rowi = lax.broadcasted_iota(jnp.int32, (tm, GW), 0)
    k = 1
    while k < tm:
        keep = rowi >= k
        bt = jnp.where(keep, a * pltpu.roll(bt, k, 0) + bt, bt)
        a = jnp.where(keep, a * pltpu.roll(a, k, 0), a)
        k *= 2
    hs = a * hc_ref[...] + bt
    hc_ref[...] = hs[tm - 1:tm]
    nh_ref[0] = hs[tm - 1:tm]
    abd_ref[0, :, GW:2 * GW] = (hs * _gelu(proj[:, OFF_GB:OFF_GB + GW])).astype(BF16)

    cosq = rope_ref[:, 0:HEAD_LANES]
    sinq = rope_ref[:, HEAD_LANES:2 * HEAD_LANES]
    cosk = rope_ref[:, 2 * HEAD_LANES:3 * HEAD_LANES]
    sink = rope_ref[:, 3 * HEAD_LANES:4 * HEAD_LANES]
    q = _dot((_rms(proj[:, OFF_CQ:OFF_CQ + GW]) * vec(V_QNORM)).astype(BF16), wq_ref[...])
    half = N_HEADS * HEAD_LANES
    for hh in range(N_HEADS):
        lo = hh * HEAD_LANES
        qh = q[:, lo:lo + HEAD_LANES] * cosq + q[:, half + lo:half + lo + HEAD_LANES] * sinq
        qcat_ref[0, hh] = (qh * MLA_SCALE).astype(BF16)
    ckv = _rms(proj[:, OFF_CKV:OFF_CKV + GW]) * vec(V_KVNORM)
    ckvf_ref[0] = ckv
    kv = _dot(ckv.astype(BF16), wkv_ref[...])
    kr = proj[:, OFF_KRA:OFF_KRA + HEAD_LANES] * cosk + proj[:, OFF_KRB:OFF_KRB + HEAD_LANES] * sink
    krf_ref[0] = kr[:, 0:QK_ROPE]
    kr_hi = jnp.where(lax.broadcasted_iota(jnp.int32, (1, HEAD_LANES), 1) >= QK_NOPE, kr, 0.0)
    for hh in range(N_HEADS):
        kcat_ref[0, hh] = (kv[:, hh * HEAD_LANES:(hh + 1) * HEAD_LANES] + kr_hi).astype(BF16)
    vt_ref[0, 0] = kv[:, half:half + GW].T.astype(BF16)

    pd = proj[:, OFF_PD:OFF_PD + GW]
    zp_ref[16:16 + tm, :] = pd
    zz = zp_ref[...]
    s2 = zz + pltpu.roll(zz, 1, 0)
    s4 = s2 + pltpu.roll(s2, 2, 0)
    s8 = s4 + pltpu.roll(s4, 4, 0)
    s16 = s8 + pltpu.roll(s8, 8, 0)
    wsum = _pool_select(s2, s4, s8, s16)[16:]
    pos = (t * tm + lax.broadcasted_iota(jnp.int32, (tm, 1), 0) + 1).astype(F32)
    cnt = jnp.minimum(pos, _pool_windows())
    d = (wsum / cnt - pd).astype(BF16)
    abd_ref[0, :, 2 * GW:3 * GW] = (_dot(d, wpool_ref[...]) * vec(V_PSCALE)).astype(BF16)
    npool_ref[0] = zp_ref[pl.ds(16 + tm - POOL_BUF, POOL_BUF), :]
    zp_ref[0:16, :] = pd[tm - 16:tm]


def _premix_prompt(x, mod, lw, rope):
    b, t, d = x.shape
    tm = PROMPT_TILE
    assert t % tm == 0 and tm % CHUNK == 0 and tm == ATTN_TILE
    grid = (b, t // tm)
    tok = lambda w: pl.BlockSpec((1, tm, w), lambda i, j: (i, j, 0))
    per_b = lambda r: pl.BlockSpec((1, r, GW), lambda i, j: (i, 0, 0))
    heads = pl.BlockSpec((1, N_HEADS, tm, HEAD_LANES), lambda i, j: (i, 0, j, 0))
    in_specs = [
        tok(d),
        pl.BlockSpec((1, 1, 2 * d), lambda i, j: (i, 0, 0)),
        _full(lw["w_in"].shape), _full(lw["sgu_w"].shape), _full(lw["sgu_bias"].shape),
        _full(lw["vec"].shape), _full(lw["w_gate"].shape), _full(lw["b_gate"].shape),
        _full(lw["w_qh"].shape), _full(lw["w_kv"].shape), _full(lw["w_pool"].shape),
        pl.BlockSpec((tm, 4 * HEAD_LANES), lambda i, j: (j, 0)),
    ]
    out_specs = [
        tok(3 * GW), heads, heads,
        pl.BlockSpec((1, 1, GW, tm), lambda i, j: (i, j, 0, 0)),
        tok(GW), tok(QK_ROPE),
        per_b(CONV_WIDTH - 1), per_b(1), per_b(POOL_BUF),
    ]
    out_shape = [
        jax.ShapeDtypeStruct((b, t, 3 * GW), BF16),
        jax.ShapeDtypeStruct((b, N_HEADS, t, HEAD_LANES), BF16),
        jax.ShapeDtypeStruct((b, N_HEADS, t, HEAD_LANES), BF16),
        jax.ShapeDtypeStruct((b, t // tm, GW, tm), BF16),
        jax.ShapeDtypeStruct((b, t, GW), F32),
        jax.ShapeDtypeStruct((b, t, QK_ROPE), F32),
        jax.ShapeDtypeStruct((b, CONV_WIDTH - 1, GW), F32),
        jax.ShapeDtypeStruct((b, 1, GW), F32),
        jax.ShapeDtypeStruct((b, POOL_BUF, GW), F32),
    ]
    return pl.pallas_call(
        functools.partial(_premix_prompt_kernel, tm=tm),
        grid=grid, in_specs=in_specs, out_specs=out_specs, out_shape=out_shape,
        scratch_shapes=[pltpu.VMEM((tm + 8, GW), F32), pltpu.VMEM((tm + 16, GW), F32),
                        pltpu.VMEM((1, GW), F32)],
        compiler_params=_cparams(("arbitrary", "arbitrary")),
        name="premix_prompt",
    )(x, mod, lw["w_in"], lw["sgu_w"], lw["sgu_bias"], lw["vec"], lw["w_gate"], lw["b_gate"],
      lw["w_qh"], lw["w_kv"], lw["w_pool"], rope)


def _attn_prompt_kernel(q_ref, k_ref, vt_ref, o_ref, m_ref, l_ref, acc_ref, s_ref, *, tq):
    qi = pl.program_id(1)
    cols = N_HEADS * tq
    m_ref[...] = jnp.full((1, cols), -jnp.inf, F32)
    l_ref[...] = jnp.zeros((1, cols), F32)
    acc_ref[...] = jnp.zeros((GW, tq), F32)
    qs = [q_ref[0, hh] for hh in range(N_HEADS)]

    def scores(j):
        start = pl.multiple_of(j * tq, tq)
        return jnp.concatenate(
            [_dot_nt(k_ref[0, hh, pl.ds(start, tq), :], qs[hh]) for hh in range(N_HEADS)], 1)

    def accumulate(j, st):
        m_old = m_ref[...]
        m_new = jnp.maximum(m_old, jnp.max(st, 0, keepdims=True))
        alpha = jnp.exp(m_old - m_new)
        p = jnp.exp(st - m_new)
        l_ref[...] = alpha * l_ref[...] + jnp.sum(p, 0, keepdims=True)
        m_ref[...] = m_new
        pb = p.astype(BF16)
        vt = vt_ref[0, j]
        for hh in range(N_HEADS):
            r = slice(hh * HEAD_DIM, (hh + 1) * HEAD_DIM)
            c = slice(hh * tq, (hh + 1) * tq)
            acc_ref[r, :] = alpha[:, c] * acc_ref[r, :] + _dot(vt[r, :], pb[:, c])

    s_ref[...] = scores(0)

    def body(j, carry):
        st = s_ref[...]
        s_next = scores(j + 1)
        accumulate(j, st)
        s_ref[...] = s_next
        return carry

    lax.fori_loop(0, qi, body, 0)
    kpos = lax.broadcasted_iota(jnp.int32, (tq, cols), 0)
    qpos = _mod(lax.broadcasted_iota(jnp.int32, (tq, cols), 1), tq)
    accumulate(qi, jnp.where(kpos <= qpos, s_ref[...], -jnp.inf))
    linv = 1.0 / l_ref[...]
    out = jnp.concatenate(
        [acc_ref[hh * HEAD_DIM:(hh + 1) * HEAD_DIM, :] * linv[:, hh * tq:(hh + 1) * tq] for hh in range(N_HEADS)], 0)
    o_ref[0] = out.T.astype(BF16)


def _attn_prompt(qcat, kcat, vt):
    b, _, t, _ = qcat.shape
    tq = ATTN_TILE
    assert t % tq == 0
    return pl.pallas_call(
        functools.partial(_attn_prompt_kernel, tq=tq),
        grid=(b, t // tq),
        in_specs=[pl.BlockSpec((1, N_HEADS, tq, HEAD_LANES), lambda i, j: (i, 0, j, 0)),
                  pl.BlockSpec((1, N_HEADS, t, HEAD_LANES), lambda i, j: (i, 0, 0, 0)),
                  pl.BlockSpec((1, t // tq, GW, tq), lambda i, j: (i, 0, 0, 0))],
        out_specs=pl.BlockSpec((1, tq, GW), lambda i, j: (i, j, 0)),
        out_shape=jax.ShapeDtypeStruct((b, t, GW), BF16),
        scratch_shapes=[pltpu.VMEM((1, N_HEADS * tq), F32), pltpu.VMEM((1, N_HEADS * tq), F32),
                        pltpu.VMEM((GW, tq), F32), pltpu.VMEM((tq, N_HEADS * tq), F32)],
        compiler_params=_cparams(("arbitrary", "arbitrary")),
        name="attn_prompt",
    )(qcat, kcat, vt)


def _premix_sample_kernel(x_ref, mod_ref, win_ref, sguc_ref, sgub_ref, vec_ref, wgate_ref, bgate_ref,
                          wq_ref, wuk_ref, wpool_ref, cos_ref, sin_ref, cbuf_ref, h0_ref, pbuf_ref,
                          abd_ref, qlat_ref, qrope_ref, ckvf_ref, ckvb_ref, krf_ref, krb_ref, vrows_ref,
                          nconv_ref, nh_ref, npool_ref, *, steps, start):
    def vec(i):
        return vec_ref[i:i + 1, :]

    sh1 = mod_ref[:, 0:D_MODEL]
    sc1 = mod_ref[:, D_MODEL:2 * D_MODEL]
    win = win_ref[...]
    projs = [_dot((_ln(x_ref[t]) * (1.0 + sc1) + sh1).astype(BF16), win) for t in range(steps)]

    vgs = []
    for t in range(steps):
        vg = _ln(_gelu(projs[t][:, OFF_V:OFF_V + GW])) * vec(V_SGU_G) + vec(V_SGU_B)
        vrows_ref[t] = vg
        vgs.append(vg)
    for t in range(steps):
        mixed = sgub_ref[t:t + 1, :]
        for s in range(t + 1):
            mixed = mixed + sguc_ref[t, s:s + 1, :] * vgs[s]
        abd_ref[t, :, 0:GW] = (_gelu(projs[t][:, OFF_U:OFF_U + GW]) * mixed).astype(BF16)

    z = [cbuf_ref[k] for k in range(CONV_WIDTH - 1)] + [p[:, OFF_XB:OFF_XB + GW] for p in projs]
    hstate = h0_ref[...]
    for t in range(steps):
        xc = vec(V_CONV_B) + z[t] * vec(V_CONV_W)
        for k in range(1, CONV_WIDTH):
            xc = xc + z[t + k] * vec(V_CONV_W + k)
        a, bt = _lru_gates(xc, wgate_ref[...], bgate_ref[...], vec(V_LAM))
        hstate = a * hstate + bt
        abd_ref[t, :, GW:2 * GW] = (hstate * _gelu(projs[t][:, OFF_GB:OFF_GB + GW])).astype(BF16)
    nh_ref[...] = hstate
    for k in range(CONV_WIDTH - 1):
        nconv_ref[k] = z[steps + k]

    for t in range(steps):
        cos = cos_ref[t:t + 1, :]
        sin = sin_ref[t:t + 1, :]
        p = projs[t]
        qlat, qr = _mla_queries(p[:, OFF_CQ:OFF_CQ + GW], vec(V_QNORM), wq_ref[...], wuk_ref[...], cos, sin)
        qlat_ref[t] = qlat.astype(BF16)
        qrope_ref[t] = qr.astype(BF16)
        ckv = _rms(p[:, OFF_CKV:OFF_CKV + GW]) * vec(V_KVNORM)
        ckvf_ref[t] = ckv
        ckvb_ref[t] = ckv.astype(BF16)
        kr = (p[:, OFF_KRA:OFF_KRA + ROPE_LANES] * cos + p[:, OFF_KRB:OFF_KRB + ROPE_LANES] * sin)[:, 0:QK_ROPE]
        krf_ref[t] = kr
        krb_ref[t] = kr.astype(BF16)

    zp = [pbuf_ref[k] for k in range(POOL_BUF)] + [p[:, OFF_PD:OFF_PD + GW] for p in projs]
    wins = _pool_windows()
    for t in range(steps):
        sums = []
        run = zp[POOL_BUF + t]
        for j in range(1, POOL_WINDOWS[-1]):
            if j in POOL_WINDOWS:
                sums.append(run)
            run = run + zp[POOL_BUF + t - j]
        sums.append(run)
        cnt = jnp.minimum(float(start + t + 1), wins)
        d = (_pool_select(*sums) / cnt - zp[POOL_BUF + t]).astype(BF16)
        abd_ref[t, :, 2 * GW:3 * GW] = (_dot(d, wpool_ref[...]) * vec(V_PSCALE)).astype(BF16)
    for k in range(POOL_BUF):
        npool_ref[k] = zp[steps + k]


def _premix_sample(x_tm, mod, lw, sgu_coef, sgu_bias, cos, sin, cbuf, h0, pbuf, start):
    steps, bd, d = x_tm.shape
    assert steps <= min(CHUNK, NEW_KEY_PAD)
    f = lambda w, dt: jax.ShapeDtypeStruct((steps, bd, w), dt)
    out_shape = [f(3 * GW, BF16), f(N_HEADS * GW, BF16), f(ROPE_LANES, BF16), f(GW, F32), f(GW, BF16),
                 f(QK_ROPE, F32), f(QK_ROPE, BF16), f(GW, F32),
                 jax.ShapeDtypeStruct((CONV_WIDTH - 1, bd, GW), F32),
                 jax.ShapeDtypeStruct((bd, GW), F32),
                 jax.ShapeDtypeStruct((POOL_BUF, bd, GW), F32)]
    args = (x_tm, mod, lw["w_in"], sgu_coef, sgu_bias, lw["vec"], lw["w_gate"], lw["b_gate"],
            lw["w_q"], lw["w_uk"], lw["w_pool"], cos, sin, cbuf, h0, pbuf)
    in_specs = [_full(a.shape) for a in args]
    in_specs[1] = pl.BlockSpec((bd, 2 * d), lambda i: (0, 0))
    return pl.pallas_call(
        functools.partial(_premix_sample_kernel, steps=steps, start=start),
        grid=(1,), in_specs=in_specs,
        out_specs=[_full(s.shape) for s in out_shape], out_shape=out_shape,
        compiler_params=_cparams(("arbitrary",)),
        name="premix_sample",
    )(*args)


def _decode_kernel(pt_ref, ql_ref, qr_ref, nk_ref, nkr_ref, ckv_hbm, krt_hbm, o_ref,
                   m_ref, l_ref, acc_ref, kb_ref, sb_ref, kin_ref, krin_ref, sem, *, layer, pages, steps):
    b = pl.program_id(0)
    j = pl.program_id(1)
    nj = pl.num_programs(1)
    rows = N_HEADS * steps
    slot = j & 1
    g = b * nj + j
    land = lax.rem(g, DECODE_RING)
    total = pl.num_programs(0) * nj

    def page_copies(seq, grp, buf):
        out = []
        for i in range(pages):
            page = pt_ref[seq, grp * pages + i]
            out.append(pltpu.make_async_copy(ckv_hbm.at[layer, page], kin_ref.at[buf, i], sem.at[buf]))
            out.append(pltpu.make_async_copy(krt_hbm.at[layer, page], krin_ref.at[buf, i], sem.at[buf]))
        return out

    def start_step(step):
        for c in page_copies(lax.div(step, nj), lax.rem(step, nj), lax.rem(step, DECODE_RING)):
            c.start()

    for ahead in range(DECODE_RING - 1):
        @pl.when((g == 0) & (ahead < total))
        def _():
            start_step(jnp.int32(ahead))

    @pl.when(g + (DECODE_RING - 1) < total)
    def _():
        start_step(g + (DECODE_RING - 1))

    for c in page_copies(b, j, land):
        c.wait()

    @pl.when(j == 0)
    def _():
        m_ref[...] = jnp.full((rows, 1), MASK_FLOOR, F32)
        l_ref[...] = jnp.zeros((rows, 1), F32)
        acc_ref[...] = jnp.zeros((rows, GW), F32)
        sb_ref[...] = jnp.full(sb_ref.shape, -jnp.inf, F32)
        kb_ref[1] = jnp.zeros(kb_ref.shape[1:], BF16)

    ql = ql_ref[0]
    qr = qr_ref[0]

    def update(s, values):
        m_old = m_ref[...]
        m_new = jnp.maximum(m_old, jnp.max(s, -1, keepdims=True))
        alpha = jnp.exp(m_old - m_new)
        p = jnp.exp(s - m_new)
        l_ref[...] = alpha * l_ref[...] + jnp.sum(p, -1, keepdims=True)
        acc_ref[...] = alpha * acc_ref[...] + _dot(p.astype(BF16), values)
        m_ref[...] = m_new

    parts = []
    for i in range(pages):
        k = kin_ref[land, i].astype(BF16)
        kb_ref[slot, i * PAGE_SIZE:(i + 1) * PAGE_SIZE, :] = k
        parts.append(_dot_nt(ql, k) + _dot(qr, krin_ref[land, i].astype(BF16)))
    s_now = jnp.concatenate(parts, -1)
    update(sb_ref[...], kb_ref[1 - slot])
    sb_ref[...] = s_now

    @pl.when(j == pl.num_programs(1) - 1)
    def _():
        update(sb_ref[...], kb_ref[slot])
        nk = nk_ref[0]
        s_new = _dot_nt(ql, nk) + _dot_nt(qr, nkr_ref[0])
        qstep = _mod(lax.broadcasted_iota(jnp.int32, (rows, NEW_KEY_PAD), 0), steps)
        kstep = lax.broadcasted_iota(jnp.int32, (rows, NEW_KEY_PAD), 1)
        update(jnp.where(kstep <= qstep, s_new, -jnp.inf), nk)
        o_ref[0] = acc_ref[...] / l_ref[...]


def _decode_attn(layer, page_table, cache_ckv, cache_krt, qlat, qrope, nk, nkr):
    bd, rows, _ = qlat.shape
    n_pages = page_table.shape[1]
    pages = min(DECODE_PAGES, n_pages)
    assert n_pages % pages == 0
    steps = rows // N_HEADS

    per_seq = lambda r, w: pl.BlockSpec((1, r, w), lambda i, j, pt: (i, 0, 0))
    in_specs = [per_seq(rows, GW), per_seq(rows, QK_ROPE), per_seq(NEW_KEY_PAD, GW), per_seq(NEW_KEY_PAD, QK_ROPE),
                pl.BlockSpec(memory_space=pl.ANY), pl.BlockSpec(memory_space=pl.ANY)]
    grid_spec = pltpu.PrefetchScalarGridSpec(
        num_scalar_prefetch=1, grid=(bd, n_pages // pages),
        in_specs=in_specs, out_specs=per_seq(rows, GW),
        scratch_shapes=[pltpu.VMEM((rows, 1), F32), pltpu.VMEM((rows, 1), F32), pltpu.VMEM((rows, GW), F32),
                        pltpu.VMEM((2, pages * PAGE_SIZE, GW), BF16),
                        pltpu.VMEM((rows, pages * PAGE_SIZE), F32),
                        pltpu.VMEM((DECODE_RING, pages, PAGE_SIZE, GW), F32),
                        pltpu.VMEM((DECODE_RING, pages, QK_ROPE, PAGE_SIZE), F32),
                        pltpu.SemaphoreType.DMA((DECODE_RING,))])
    return pl.pallas_call(
        functools.partial(_decode_kernel, layer=layer, pages=pages, steps=steps),
        grid_spec=grid_spec,
        out_shape=jax.ShapeDtypeStruct((bd, rows, GW), F32),
        compiler_params=_cparams(("arbitrary", "arbitrary")),
        name="decode_attn",
    )(page_table, qlat, qrope, nk, nkr, cache_ckv, cache_krt)


def _uv_kernel(o_ref, wuv_ref, out_ref):
    out = _dot(o_ref[0].astype(BF16), wuv_ref[0])
    for hh in range(1, N_HEADS):
        out = out + _dot(o_ref[hh].astype(BF16), wuv_ref[hh])
    out_ref[...] = out.astype(BF16)


def _uv_project(o_heads, wuv):
    _, n, _ = o_heads.shape
    return pl.pallas_call(
        _uv_kernel, grid=(1,),
        in_specs=[_full(o_heads.shape), _full(wuv.shape)],
        out_specs=_full((n, GW)), out_shape=jax.ShapeDtypeStruct((n, GW), BF16),
        compiler_params=_cparams(("arbitrary",)), name="uv_project",
    )(o_heads, wuv)


def _mix_norm(x_ref, abd_ref, c_ref, moda_ref, modb_ref, wabd_ref, wc_ref, lnv_ref):
    mix = _dot(abd_ref[...], wabd_ref[...]) + _dot(c_ref[...], wc_ref[...])
    g1 = moda_ref[0, :, 0:D_MODEL]
    sh2 = moda_ref[0, :, D_MODEL:2 * D_MODEL]
    sc2 = modb_ref[0, :, 0:D_MODEL]
    x1 = _ln(ALPHA * x_ref[...] + g1 * mix) * lnv_ref[0:1, :] + lnv_ref[1:2, :]
    h2 = (_ln(x1) * (1.0 + sc2) + sh2).astype(BF16)
    return x1, h2


def _swiglu(h2, w1, w3, w2):
    return _dot((_silu(_dot(h2, w1)) * _dot(h2, w3)).astype(BF16), w2)


def _post_dense_kernel(x_ref, abd_ref, c_ref, moda_ref, modb_ref, wabd_ref, wc_ref, lnv_ref,
                       w1_ref, w3_ref, w2_ref, o_ref):
    x1, h2 = _mix_norm(x_ref, abd_ref, c_ref, moda_ref, modb_ref, wabd_ref, wc_ref, lnv_ref)
    f = _swiglu(h2, w1_ref[...], w3_ref[...], w2_ref[...])
    g2 = modb_ref[0, :, D_MODEL:2 * D_MODEL]
    o_ref[...] = _ln(ALPHA * x1 + g2 * f) * lnv_ref[2:3, :] + lnv_ref[3:4, :]


def _token_specs(tm, tiles_per_group, mod_rows):
    mi = lambda blk: (lambda i, *_: (i // tiles_per_group, 0, blk))
    tok = lambda w: pl.BlockSpec((tm, w), lambda i, *_: (i, 0))
    mod = lambda blk: pl.BlockSpec((1, mod_rows, 2 * D_MODEL), mi(blk))
    return tok, mod


def _post_dense(x, abd, c, mod, lw, fw, tm, tiles_per_group):
    n, d = x.shape
    tok, modspec = _token_specs(tm, tiles_per_group, mod.shape[1])
    weights = (lw["w_abd"], lw["w_c"], lw["lnv"], fw["w1"], fw["w3"], fw["w2"])
    return pl.pallas_call(
        _post_dense_kernel, grid=(n // tm,),
        in_specs=[tok(d), tok(3 * GW), tok(GW), modspec(1), modspec(2)] + [_full(w.shape) for w in weights],
        out_specs=tok(d), out_shape=jax.ShapeDtypeStruct((n, d), F32),
        compiler_params=_cparams(("arbitrary",)), name="post_dense",
    )(x, abd, c, mod, mod, *weights)


def _post_route_kernel(x_ref, abd_ref, c_ref, moda_ref, modb_ref, wabd_ref, wc_ref, lnv_ref,
                       wr_ref, br_ref, x1_ref, h2_ref, route_ref, cnt_ref, run_ref):
    tm = x_ref.shape[0]

    @pl.when(pl.program_id(0) == 0)
    def _():
        run_ref[...] = jnp.zeros((1, 128), F32)

    x1, h2 = _mix_norm(x_ref, abd_ref, c_ref, moda_ref, modb_ref, wabd_ref, wc_ref, lnv_ref)
    x1_ref[...] = x1
    h2_ref[...] = h2
    lane = lax.broadcasted_iota(jnp.int32, (1, 128), 1)
    logits = jnp.where(lane < N_EXPERTS, _dot(h2, wr_ref[...]) + br_ref[...], -jnp.inf)
    m1 = jnp.max(logits, -1, keepdims=True)
    i1 = jnp.min(jnp.where(logits == m1, lane, 128), -1, keepdims=True)
    rest = jnp.where(lane == i1, -jnp.inf, logits)
    m2 = jnp.max(rest, -1, keepdims=True)
    i2 = jnp.min(jnp.where(rest == m2, lane, 128), -1, keepdims=True)
    e2 = jnp.exp(m2 - m1)
    gate1 = 1.0 / (1.0 + e2)
    gate2 = e2 / (1.0 + e2)
    sel = jnp.where((lane == i1) | (lane == i2), 1.0, 0.0)
    r = lax.broadcasted_iota(jnp.int32, (tm, tm), 0)
    c = lax.broadcasted_iota(jnp.int32, (tm, tm), 1)
    before = _dot(jnp.where(c < r, 1.0, 0.0).astype(BF16), sel.astype(BF16)) + run_ref[...]
    rank1 = jnp.sum(jnp.where(lane == i1, before, 0.0), -1, keepdims=True)
    rank2 = jnp.sum(jnp.where(lane == i2, before, 0.0), -1, keepdims=True)
    run_ref[...] = run_ref[...] + jnp.sum(sel, 0, keepdims=True)
    cnt_ref[...] = run_ref[...]
    cols = (i1.astype(F32), i2.astype(F32), gate1, gate2, rank1, rank2)
    route = jnp.zeros((tm, 128), F32)
    for k, v in enumerate(cols):
        route = jnp.where(lane == k, v, route)
    route_ref[...] = route


def _post_route(x, abd, c, mod, lw, mw, tm, tiles_per_group):
    n, d = x.shape
    tok, modspec = _token_specs(tm, tiles_per_group, mod.shape[1])
    weights = (lw["w_abd"], lw["w_c"], lw["lnv"], mw["w_router"], mw["b_router"])
    return pl.pallas_call(
        _post_route_kernel, grid=(n // tm,),
        in_specs=[tok(d), tok(3 * GW), tok(GW), modspec(1), modspec(2)] + [_full(w.shape) for w in weights],
        out_specs=[tok(d), tok(d), tok(128), _full((1, 128))],
        out_shape=[jax.ShapeDtypeStruct((n, d), F32), jax.ShapeDtypeStruct((n, d), BF16),
                   jax.ShapeDtypeStruct((n, 128), F32), jax.ShapeDtypeStruct((1, 128), F32)],
        scratch_shapes=[pltpu.VMEM((1, 128), F32)],
        compiler_params=_cparams(("arbitrary",)), name="post_route",
    )(x, abd, c, mod, mod, *weights)


def _moe_group_kernel(tile_ref, blk_ref, first_ref, last_ref, valid_ref, texp_ref,
                      h2_ref, post_ref, w1_ref, w3_ref, w2_ref, out_ref, xg_ref):
    del blk_ref, texp_ref
    w = pl.program_id(0)

    @pl.when(first_ref[w] == 1)
    def _():
        xg_ref[...] = jnp.zeros_like(xg_ref)

    @pl.when(valid_ref[w] == 1)
    def _():
        row = tile_ref[w] * MOE_TILE + lax.broadcasted_iota(jnp.int32, (MOE_TILE, 1), 0)
        pick = (post_ref[0:1, :] == row) | (post_ref[1:2, :] == row)
        xg_ref[...] += _dot(jnp.where(pick, 1.0, 0.0).astype(BF16), h2_ref[...])

    @pl.when(last_ref[w] == 1)
    def _():
        out_ref[...] = _swiglu(xg_ref[...].astype(BF16), w1_ref[...], w3_ref[...], w2_ref[...]).astype(BF16)

    @pl.when(valid_ref[w] == 0)
    def _():
        out_ref[...] = jnp.zeros_like(out_ref)


def _moe_group(items, item_expert, h2, pos_t, mw, n_rows):
    n, d = h2.shape
    n_items = items[0].shape[0]
    wspec = lambda shape: pl.BlockSpec((None,) + shape, lambda w, tile, blk, f, l, v, te: (te[w], 0, 0))
    grid_spec = pltpu.PrefetchScalarGridSpec(
        num_scalar_prefetch=6, grid=(n_items,),
        in_specs=[pl.BlockSpec((MOE_BLOCK, d), lambda w, tile, blk, f, l, v, te: (blk[w], 0)),
                  pl.BlockSpec((2, MOE_BLOCK), lambda w, tile, blk, f, l, v, te: (0, blk[w])),
                  wspec((d, D_FF)), wspec((d, D_FF)), wspec((D_FF, d))],
        out_specs=pl.BlockSpec((MOE_TILE, d), lambda w, tile, blk, f, l, v, te: (tile[w], 0)),
        scratch_shapes=[pltpu.VMEM((MOE_TILE, d), F32)])
    return pl.pallas_call(
        _moe_group_kernel, grid_spec=grid_spec,
        out_shape=jax.ShapeDtypeStruct((n_rows, d), BF16),
        compiler_params=_cparams(("arbitrary",)), name="moe_group",
    )(*items, item_expert, h2, pos_t, mw["w1"], mw["w3"], mw["w2"])


def _moe_combine_kernel(blk_ref, tile_ref, first_ref, last_ref, valid_ref,
                        out_ref, pos_ref, gate_ref, y_ref, acc_ref):
    del blk_ref
    w = pl.program_id(0)

    @pl.when(first_ref[w] == 1)
    def _():
        acc_ref[...] = jnp.zeros_like(acc_ref)

    @pl.when(valid_ref[w] == 1)
    def _():
        row = tile_ref[w] * MOE_TILE + lax.broadcasted_iota(jnp.int32, (1, MOE_TILE), 1)
        sel = (jnp.where(pos_ref[:, 0:1] == row, gate_ref[:, 0:1], 0.0)
               + jnp.where(pos_ref[:, 1:2] == row, gate_ref[:, 1:2], 0.0))
        acc_ref[...] += _dot(sel.astype(BF16), out_ref[...])

    @pl.when(last_ref[w] == 1)
    def _():
        y_ref[...] = acc_ref[...]


def _moe_combine(items, out_sorted, pos, gates):
    n = pos.shape[0]
    d = out_sorted.shape[1]
    n_items = items[0].shape[0]
    tokb = lambda w_: pl.BlockSpec((MOE_BLOCK, w_), lambda w, blk, tile, f, l, v: (blk[w], 0))
    grid_spec = pltpu.PrefetchScalarGridSpec(
        num_scalar_prefetch=5, grid=(n_items,),
        in_specs=[pl.BlockSpec((MOE_TILE, d), lambda w, blk, tile, f, l, v: (tile[w], 0)), tokb(2), tokb(2)],
        out_specs=tokb(d),
        scratch_shapes=[pltpu.VMEM((MOE_BLOCK, d), F32)])
    return pl.pallas_call(
        _moe_combine_kernel, grid_spec=grid_spec,
        out_shape=jax.ShapeDtypeStruct((n, d), F32),
        compiler_params=_cparams(("arbitrary",)), name="moe_combine",
    )(*items, out_sorted, pos, gates)


def _moe_final_kernel(x1_ref, y_ref, modb_ref, lnv_ref, o_ref):
    g2 = modb_ref[0, :, D_MODEL:2 * D_MODEL]
    o_ref[...] = _ln(ALPHA * x1_ref[...] + g2 * y_ref[...]) * lnv_ref[2:3, :] + lnv_ref[3:4, :]


def _moe_final(x1, y, y_row0, mod, lw, tm, tiles_per_group):
    n, d = x1.shape
    assert y_row0 % tm == 0
    tok, modspec = _token_specs(tm, tiles_per_group, mod.shape[1])
    return pl.pallas_call(
        _moe_final_kernel, grid=(n // tm,),
        in_specs=[tok(d), pl.BlockSpec((tm, d), lambda i: (i + y_row0 // tm, 0)), modspec(2),
                  _full(lw["lnv"].shape)],
        out_specs=tok(d), out_shape=jax.ShapeDtypeStruct((n, d), F32),
        compiler_params=_cparams(("arbitrary",)), name="moe_final",
    )(x1, y, mod, lw["lnv"])


def _pair_items(incidence, n_items):
    n_cols = incidence.shape[1]
    running = jnp.cumsum(incidence.reshape(-1).astype(jnp.int32))
    n_valid = running[-1]
    idx = jnp.arange(n_items)
    flat = jnp.minimum(jnp.searchsorted(running, idx + 1, side="left", method="compare_all"), running.shape[0] - 1)
    rows, cols = flat // n_cols, flat % n_cols
    valid = idx < n_valid
    last_valid = jnp.maximum(n_valid - 1, 0)
    rows = jnp.where(valid, rows, rows[last_valid]).astype(jnp.int32)
    cols = jnp.where(valid, cols, cols[last_valid]).astype(jnp.int32)
    prev_rows = jnp.concatenate([rows[:1] - 1, rows[:-1]])
    next_rows = jnp.concatenate([rows[1:], rows[-1:] + 1])
    first = valid & (rows != prev_rows)
    last = valid & ((rows != next_rows) | (idx == last_valid))
    return rows, cols, first.astype(jnp.int32), last.astype(jnp.int32), valid.astype(jnp.int32)


def _moe_ffn(h2, route, counts, mw):
    n, d = h2.shape
    assert n % MOE_BLOCK == 0
    n_blocks = n // MOE_BLOCK
    n_tiles = (2 * n + N_EXPERTS * (MOE_TILE - 1)) // MOE_TILE + 1
    n_items = n_tiles + N_EXPERTS * n_blocks
    experts = route[:, 0:2].astype(jnp.int32)
    gates = route[:, 2:4]
    tiles_per_expert = (counts.astype(jnp.int32) + MOE_TILE - 1) // MOE_TILE
    tile_end = jnp.cumsum(tiles_per_expert)
    row_start = (tile_end - tiles_per_expert) * MOE_TILE
    pos = row_start[experts] + route[:, 4:6].astype(jnp.int32)
    tile_expert = jnp.minimum(jnp.searchsorted(tile_end, jnp.arange(n_tiles), side="right"),
                              N_EXPERTS - 1).astype(jnp.int32)
    pick_tile = (pos // MOE_TILE).reshape(n_blocks, 1, 2 * MOE_BLOCK)
    incidence = jnp.any(pick_tile == jnp.arange(n_tiles)[None, :, None], -1)
    tile_i, blk_i, first_i, last_i, valid_i = _pair_items(incidence.T, n_items)
    item_expert = tile_expert[tile_i]
    n_valid = jnp.sum(valid_i)
    spare_tile = jnp.minimum(tile_end[-1] + jnp.arange(n_items) - n_valid, n_tiles - 1)
    tile_i = jnp.where(valid_i == 1, tile_i, spare_tile).astype(jnp.int32)
    out_sorted = _moe_group((tile_i, blk_i, first_i, last_i, valid_i), item_expert, h2, pos.T, mw,
                            n_tiles * MOE_TILE)
    return _moe_combine(_pair_items(incidence, n_items), out_sorted, pos, gates)


def _block_diag(blocks):
    n, r, c = blocks.shape
    on_diag = jnp.eye(n, dtype=bool)[:, None, :, None]
    return jnp.where(on_diag, blocks[:, :, None, :], 0).reshape(n * r, n * c)


def _swap_rope_halves(w):
    half = QK_ROPE // 2
    return jnp.concatenate([w[..., half:], w[..., :half]], -1)


def _layer_weights(l, p):
    w_in = p["w_in"][l]
    s = np.cumsum([GW, GW, GW, GW, GW, GW, QK_ROPE])
    kr_cols = w_in[:, s[5]:s[6]]
    gap = jnp.zeros_like(kr_cols)
    kr_sw = _swap_rope_halves(kr_cols)
    w_in_r = jnp.concatenate(
        [w_in[:, :s[5]], w_in[:, s[6]:], kr_cols, gap, kr_cols, gap, kr_sw, gap, kr_sw, gap], 1).astype(BF16)
    vec_names = ("sgu_ln_g", "sgu_ln_b", "lru_conv_b", "lru_lambda", "mla_q_norm", "mla_kv_norm", "pool_scale")
    assert (V_SGU_G, V_SGU_B, V_CONV_B, V_LAM, V_QNORM, V_KVNORM, V_PSCALE) == tuple(range(len(vec_names)))
    vec = jnp.concatenate(
        [jnp.stack([p[name][l] for name in vec_names]), jnp.zeros((V_CONV_W - len(vec_names), GW), F32),
         p["lru_conv_w"][l], jnp.zeros((VEC_ROWS - V_CONV_W - CONV_WIDTH, GW), F32)], 0)
    w_uq = p["mla_w_uq"][l]
    q_rope_cols = w_uq[:, :, QK_NOPE:]
    w_q = jnp.concatenate([w_uq[:, :, :QK_NOPE].reshape(GW, GW), q_rope_cols.reshape(GW, ROPE_LANES),
                           _swap_rope_halves(q_rope_cols).reshape(GW, ROPE_LANES)], 1).astype(BF16)
    w_uk = _block_diag(jnp.transpose(p["mla_w_uk"][l], (1, 2, 0))).astype(BF16)
    w_uv = p["mla_w_uv"][l]
    zq = jnp.zeros((GW, N_HEADS, HEAD_LANES - QK_NOPE - QK_ROPE), F32)
    w_qh = jnp.concatenate(
        [jnp.concatenate([w_uq, zq], -1).reshape(GW, N_HEADS * HEAD_LANES),
         jnp.concatenate([jnp.zeros((GW, N_HEADS, QK_NOPE), F32), _swap_rope_halves(q_rope_cols), zq],
                         -1).reshape(GW, N_HEADS * HEAD_LANES)], 1).astype(BF16)
    w_kn = jnp.concatenate([p["mla_w_uk"][l], jnp.zeros((GW, N_HEADS, HEAD_LANES - QK_NOPE), F32)], -1)
    w_kv = jnp.concatenate([w_kn.reshape(GW, N_HEADS * HEAD_LANES), w_uv.reshape(GW, GW)], 1).astype(BF16)
    w_uv_pad = jnp.where(jnp.eye(N_HEADS, dtype=bool)[:, None, :, None], w_uv[None], 0).reshape(N_HEADS, GW, GW)
    w_out = p["w_out"][l]
    return {
        "w_in": w_in_r,
        "sgu_w": p["sgu_w"][l],
        "sgu_b": p["sgu_b"][l],
        "vec": vec,
        "w_gate": jnp.concatenate([_block_diag(p["lru_wa"][l]), _block_diag(p["lru_wx"][l])], 1).astype(BF16),
        "b_gate": jnp.concatenate([p["lru_ba"][l], p["lru_bx"][l]])[None, :],
        "w_q": w_q,
        "w_uk": w_uk,
        "w_qh": w_qh,
        "w_kv": w_kv,
        "w_uv": w_uv_pad.astype(BF16),
        "w_pool": _block_diag(p["pool_w"][l]).astype(BF16),
        "w_abd": jnp.concatenate([w_out[:2 * GW], w_out[3 * GW:]], 0).astype(BF16),
        "w_c": w_out[2 * GW:3 * GW].astype(BF16),
        "lnv": jnp.stack([p["ln1_g"][l], p["ln1_b"][l], p["ln2_g"][l], p["ln2_b"][l]]),
    }


def _rope_tables(start, steps):
    half = QK_ROPE // 2
    freq = ROPE_THETA ** (-jnp.arange(half, dtype=F32) / half)
    ang = (start + jnp.arange(steps)).astype(F32)[:, None] * freq[None, :]
    cos = jnp.cos(ang)
    sin = jnp.sin(ang)
    return jnp.concatenate([cos, cos], -1), jnp.concatenate([-sin, sin], -1)


def _rope_tiled(start, steps):
    cos, sin = _rope_tables(start, steps)
    return jnp.tile(cos, (1, N_HEADS)), jnp.tile(sin, (1, N_HEADS))


def _rope_head_tables(start, steps):
    cos, sin = _rope_tables(start, steps)
    one = jnp.ones((steps, QK_NOPE), F32)
    z64 = jnp.zeros((steps, QK_NOPE), F32)
    z32 = jnp.zeros((steps, QK_ROPE), F32)
    return jnp.concatenate([one, cos, z32, z64, sin, z32, cos, z32, cos, z32, sin, z32, sin, z32], 1)


def kernel(x_prompt, x_sample, c_prompt, c_sample, cache_ckv, cache_kr, page_table, state_lru_conv, state_lru_h, state_pool, ada_w, ada_b, w_in, w_out, ln1_g, ln1_b, ln2_g, ln2_b, sgu_ln_g, sgu_ln_b, sgu_w, sgu_b, lru_conv_w, lru_conv_b, lru_wa, lru_ba, lru_wx, lru_bx, lru_lambda, mla_q_norm, mla_w_uq, mla_kv_norm, mla_w_uk, mla_w_uv, pool_w, pool_scale, ffn_w1, ffn_w3, ffn_w2, moe_router, moe_router_b, moe_w1, moe_w3, moe_w2):
    p = dict(w_in=w_in, w_out=w_out, ln1_g=ln1_g, ln1_b=ln1_b, ln2_g=ln2_g, ln2_b=ln2_b,
             sgu_ln_g=sgu_ln_g, sgu_ln_b=sgu_ln_b, sgu_w=sgu_w, sgu_b=sgu_b, lru_conv_w=lru_conv_w,
             lru_conv_b=lru_conv_b, lru_wa=lru_wa, lru_ba=lru_ba, lru_wx=lru_wx, lru_bx=lru_bx,
             lru_lambda=lru_lambda, mla_q_norm=mla_q_norm, mla_w_uq=mla_w_uq, mla_kv_norm=mla_kv_norm,
             mla_w_uk=mla_w_uk, mla_w_uv=mla_w_uv, pool_w=pool_w, pool_scale=pool_scale)
    bp, tp, d = x_prompt.shape
    bd, td, _ = x_sample.shape
    n_pages = page_table.shape[1]
    past_len = n_pages * PAGE_SIZE
    tmp = PROMPT_TILE

    rope_p = _rope_head_tables(0, tp)
    cos_s, sin_s = _rope_tiled(past_len, td)
    cache_krt = jnp.swapaxes(cache_kr, 2, 3)
    pad = (-(bd + bp)) % 8
    c_all = jnp.concatenate([c_sample, c_prompt, jnp.zeros((pad, d), F32)], 0)

    xp = x_prompt
    xs = jnp.transpose(x_sample, (1, 0, 2))
    sp = [[] for _ in range(5)]
    ss = [[] for _ in range(6)]
    for l in range(DEPTH):
        lw = _layer_weights(l, p)
        mod = _ada(c_all, ada_w[l].astype(BF16), ada_b[l][None, :])
        mod_s = mod[:bd]
        mod_p = mod[bd:bd + bp][:, None, :]

        (abd_p, qcat_p, kcat_p, vt_p, ckv_p, kr_p, nconv_p, nh_p, npool_p) = _premix_prompt(
            xp, mod_p, dict(lw, sgu_w=lw["sgu_w"][:, :CHUNK, :CHUNK],
                            sgu_bias=jnp.repeat(lw["sgu_b"][:, :CHUNK].T, HEAD_DIM, axis=1)), rope_p)
        c_p = _attn_prompt(qcat_p, kcat_p, vt_p)

        sgu_coef = jnp.repeat(jnp.transpose(lw["sgu_w"][:, :td, :td], (1, 2, 0)), HEAD_DIM, axis=2)
        sgu_bias_s = jnp.repeat(lw["sgu_b"][:, :td].T, HEAD_DIM, axis=1)
        (abd_s, qlat_s, qrope_s, ckv_s, ckvb_s, kr_s, krb_s, vrows_s, nconv_s, nh_s, npool_s) = _premix_sample(
            xs, mod_s, lw, sgu_coef, sgu_bias_s, cos_s, sin_s,
            jnp.transpose(state_lru_conv[l], (1, 0, 2)), state_lru_h[l],
            jnp.transpose(state_pool[l], (1, 0, 2)), past_len)
        to_rows = lambda a, w: jnp.transpose(a.reshape(td, bd, N_HEADS, w), (1, 2, 0, 3)).reshape(bd, N_HEADS * td, w)
        pad_keys = lambda a: jnp.pad(jnp.transpose(a, (1, 0, 2)), ((0, 0), (0, NEW_KEY_PAD - td), (0, 0)))
        o_s = _decode_attn(l, page_table, cache_ckv, cache_krt, to_rows(qlat_s, GW), to_rows(qrope_s, QK_ROPE),
                           pad_keys(ckvb_s), pad_keys(krb_s))
        o_heads = jnp.transpose(o_s.reshape(bd, N_HEADS, td, GW), (1, 2, 0, 3)).reshape(N_HEADS, td * bd, GW)
        c_s = _uv_project(o_heads, lw["w_uv"])

        xp2 = xp.reshape(bp * tp, d)
        xs2 = xs.reshape(td * bd, d)
        args_p = (xp2, abd_p.reshape(bp * tp, 3 * GW), c_p.reshape(bp * tp, GW), mod_p)
        args_s = (xs2, abd_s.reshape(td * bd, 3 * GW), c_s, mod_s[None])
        j = l // 2
        if l % 2 == 0:
            fw = {"w1": ffn_w1[j].astype(BF16), "w3": ffn_w3[j].astype(BF16), "w2": ffn_w2[j].astype(BF16)}
            xp2 = _post_dense(*args_p, lw, fw, tmp, tp // tmp)
            xs2 = _post_dense(*args_s, lw, fw, bd, td)
        else:
            mw = {"w_router": jnp.pad(moe_router[j], ((0, 0), (0, 128 - N_EXPERTS))).astype(BF16),
                  "b_router": jnp.pad(moe_router_b[j], (0, 128 - N_EXPERTS))[None, :],
                  "w1": moe_w1[j].astype(BF16), "w3": moe_w3[j].astype(BF16), "w2": moe_w2[j].astype(BF16)}
            x1p, h2p, route_p, cnt_p = _post_route(*args_p, lw, mw, tmp, tp // tmp)
            x1s, h2s, route_s, cnt_s = _post_route(*args_s, lw, mw, bd, td)
            before_s = cnt_p[0, :N_EXPERTS][route_s[:, 0:2].astype(jnp.int32)]
            route = jnp.concatenate([route_p, route_s.at[:, 4:6].add(before_s)], 0)
            y = _moe_ffn(jnp.concatenate([h2p, h2s], 0), route, (cnt_p + cnt_s)[0, :N_EXPERTS], mw)
            xp2 = _moe_final(x1p, y, 0, mod_p, lw, tmp, tp // tmp)
            xs2 = _moe_final(x1s, y, bp * tp, mod_s[None], lw, bd, td)
        xp = xp2.reshape(bp, tp, d)
        xs = xs2.reshape(td, bd, d)

        for k, v in enumerate((ckv_p, kr_p, nconv_p, nh_p[:, 0], npool_p)):
            sp[k].append(v)
        bt = lambda a: jnp.transpose(a, (1, 0, 2))
        for k, v in enumerate((bt(ckv_s), bt(kr_s), bt(nconv_s), nh_s, bt(npool_s), bt(vrows_s))):
            ss[k].append(v)

    return (xp, jnp.transpose(xs, (1, 0, 2)),
            jnp.stack(sp[0]), jnp.stack(sp[1]), jnp.stack(sp[2]), jnp.stack(sp[3]), jnp.stack(sp[4]),
            jnp.stack(ss[0]), jnp.stack(ss[1]), jnp.stack(ss[2]), jnp.stack(ss[3]), jnp.stack(ss[4]),
            jnp.stack(ss[5]))
```
